```python
import math
import jax, jax.numpy as jnp
from jax import lax
import numpy as np

D_MODEL = 1024
BATCH = 32
SEQ = 2048
DEPTH = 2

CHUNK = 64
CONV_K = 4
EPS = 1e-6
A_HEADS = D_MODEL // 256
A_DK = 128
A_DV = 128
A_QK_W = A_HEADS * A_DK
A_V_W = A_HEADS * A_DV
B_HEADS = D_MODEL // 256
B_DK = 128
B_DV = 128
B_QK_W = B_HEADS * B_DK
B_V_W = B_HEADS * B_DV
ROPE_BASE = 10000.0
C_INNER = D_MODEL
C_HEAD_DIM = 64
C_HEADS = C_INNER // C_HEAD_DIM
C_GROUPS = 2
C_HEADS_PER_GROUP = C_HEADS // C_GROUPS
C_STATE = 128
C_XBC_W = C_INNER + 2 * C_GROUPS * C_STATE
N_BRANCH = 3
N_GROUPS = 4
EXPERTS_PER_GROUP = 8
N_EXPERTS = N_GROUPS * EXPERTS_PER_GROUP
TOP_K_IN_GROUP = 2
D_EXPERT = D_MODEL // 2
DEEPNORM_ALPHA = (2 * DEPTH) ** 0.25
DEEPNORM_BETA = (8 * DEPTH) ** -0.25
IN_SIZES = (2 * A_QK_W + A_V_W, A_V_W, A_HEADS, A_HEADS,
            B_QK_W, B_QK_W, B_V_W, B_V_W,
            C_INNER, C_XBC_W, C_HEADS,
            N_BRANCH * D_MODEL)
P_IN = sum(IN_SIZES)

kernel_name = "hybrid_deltanet_retention_ssd_hmoe_deepnorm"


def _layernorm(x, g, b):
    xf = x.astype(jnp.float32)
    mu = jnp.mean(xf, axis=-1, keepdims=True)
    var = jnp.mean(jnp.square(xf - mu), axis=-1, keepdims=True)
    return ((xf - mu) * lax.rsqrt(var + EPS) * g + b).astype(x.dtype)


def _rmsnorm(x, g):
    xf = x.astype(jnp.float32)
    return xf * lax.rsqrt(jnp.mean(jnp.square(xf), axis=-1, keepdims=True) + EPS) * g


def _l2norm(t):
    return t * lax.rsqrt(jnp.sum(jnp.square(t), axis=-1, keepdims=True) + EPS)


def _causal_conv(x, w):
    k = w.shape[0]
    return lax.conv_general_dilated(x, w[:, None, :], window_strides=(1,), padding=[(k - 1, 0)],
                                    dimension_numbers=('NWC', 'WIO', 'NWC'),
                                    feature_group_count=x.shape[-1])


def _rope(t, cos, sin):
    half = t.shape[-1] // 2
    t1, t2 = t[..., :half], t[..., half:]
    return jnp.concatenate([t1 * cos - t2 * sin, t1 * sin + t2 * cos], axis=-1)


def _gated_deltanet(qkv, z, a, b, conv_w, a_log, dt_bias, norm_w):
    f32 = jnp.float32
    bsz, s, _ = qkv.shape
    nc = s // CHUNK
    qkv = jax.nn.silu(_causal_conv(qkv, conv_w)).astype(f32)
    q, k, v = jnp.split(qkv, [A_QK_W, 2 * A_QK_W], axis=-1)

    def heads(t, d):
        return t.reshape(bsz, nc, CHUNK, A_HEADS, d).transpose(0, 3, 1, 2, 4)

    def per_head(t):
        return t.reshape(bsz, nc, CHUNK, A_HEADS).transpose(0, 3, 1, 2)

    q = _l2norm(heads(q, A_DK)) * (A_DK ** -0.5)
    k = _l2norm(heads(k, A_DK))
    v = heads(v, A_DV)
    beta = per_head(jax.nn.sigmoid(b.astype(f32)))
    g = per_head(-jnp.exp(a_log.astype(f32)) * jax.nn.softplus(a.astype(f32) + dt_bias))
    gc = jnp.cumsum(g, axis=-1)
    strict = jnp.tril(jnp.ones((CHUNK, CHUNK), bool), -1)
    diff = gc[..., :, None] - gc[..., None, :]
    decay = jnp.where(strict, jnp.exp(jnp.where(strict, diff, 0.0)), 0.0)
    amat = beta[..., :, None] * jnp.einsum('bhcid,bhcjd->bhcij', k, k) * decay
    lhs = amat + jnp.eye(CHUNK, dtype=f32)
    rhs = jnp.concatenate([v * beta[..., None], k * (beta * jnp.exp(gc))[..., None]], axis=-1)
    sol = lax.linalg.triangular_solve(lhs, rhs, left_side=True, lower=True, unit_diagonal=True)
    u, w = sol[..., :A_DV], sol[..., A_DV:]
    g_last = gc[..., -1]
    k_end = k * jnp.exp(g_last[..., None] - gc)[..., None]

    def step(state, inp):
        q_c, k_c, u_c, w_c, dec_c = inp
        delta = u_c - jnp.einsum('bhlk,bhkv->bhlv', w_c, state)
        state = dec_c[..., None, None] * state + jnp.einsum('bhlk,bhlv->bhkv', k_c, delta)
        return state, jnp.einsum('bhlk,bhkv->bhlv', q_c, state)

    xs = (jnp.moveaxis(q, 2, 0), jnp.moveaxis(k_end, 2, 0), jnp.moveaxis(u, 2, 0),
          jnp.moveaxis(w, 2, 0), jnp.moveaxis(jnp.exp(g_last), 2, 0))
    s0 = jnp.zeros((bsz, A_HEADS, A_DK, A_DV), f32)
    _, o = lax.scan(step, s0, xs)
    o = o.transpose(1, 0, 3, 2, 4).reshape(bsz, s, A_HEADS, A_DV)
    o = _rmsnorm(o, norm_w) * jax.nn.silu(z.astype(f32).reshape(bsz, s, A_HEADS, A_DV))
    return o.reshape(bsz, s, A_V_W)


def _retention(q, k, v, gate, norm_w, cos, sin):
    f32 = jnp.float32
    bsz, s, _ = q.shape
    nc = s // CHUNK
    q = _rope(q.astype(f32).reshape(bsz, s, B_HEADS, B_DK), cos, sin)
    k = _rope(k.astype(f32).reshape(bsz, s, B_HEADS, B_DK), cos, sin) * (B_DK ** -0.5)

    def heads(t, d):
        return t.reshape(bsz, nc, CHUNK, B_HEADS, d).transpose(0, 3, 1, 2, 4)

    qh, kh, vh = heads(q, B_DK), heads(k, B_DK), heads(v.astype(f32), B_DV)
    log_gamma = jnp.log1p(-jnp.exp2(-5.0 - jnp.arange(B_HEADS, dtype=f32)))
    idx = jnp.arange(CHUNK, dtype=f32)
    intra_decay = jnp.exp(log_gamma[:, None, None] * jnp.abs(idx[:, None] - idx[None, :]))
    read_decay = jnp.exp(log_gamma[:, None] * (idx + 1.0))
    write_decay = jnp.exp(log_gamma[:, None] * (CHUNK - 1.0 - idx))
    chunk_decay = jnp.exp(log_gamma * CHUNK)
    scores = jnp.einsum('bhcid,bhcjd->bhcij', qh, kh) * intra_decay[:, None]
    o_intra = jnp.einsum('bhcij,bhcjd->bhcid', scores, vh)
    kw = kh * write_decay[:, None, :, None]

    def step(state, inp):
        q_c, k_c, v_c = inp
        o = jnp.einsum('bhlk,bhkv->bhlv', q_c, state) * read_decay[:, :, None]
        state = chunk_decay[:, None, None] * state + jnp.einsum('bhlk,bhlv->bhkv', k_c, v_c)
        return state, o

    r0 = jnp.zeros((bsz, B_HEADS, B_DK, B_DV), f32)
    _, o_inter = lax.scan(step, r0, (jnp.moveaxis(qh, 2, 0), jnp.moveaxis(kw, 2, 0), jnp.moveaxis(vh, 2, 0)))
    o = o_intra + jnp.moveaxis(o_inter, 0, 2)
    o = o.transpose(0, 2, 3, 1, 4).reshape(bsz, s, B_HEADS, B_DV)
    mu = jnp.mean(o, axis=-1, keepdims=True)
    var = jnp.mean(jnp.square(o - mu), axis=-1, keepdims=True)
    o = (o - mu) * lax.rsqrt(var + EPS) * norm_w.reshape(B_HEADS, B_DV)
    o = o * jax.nn.silu(gate.astype(f32).reshape(bsz, s, B_HEADS, B_DV))
    return o.reshape(bsz, s, B_V_W)


def _ssd(z, xbc, dt, conv_w, conv_b, dt_bias, a_log, d_skip, norm_w):
    f32 = jnp.float32
    bsz, s, _ = xbc.shape
    nc = s // CHUNK
    xbc = jax.nn.silu(_causal_conv(xbc, conv_w) + conv_b).astype(f32)
    xs, bm, cm = jnp.split(xbc, [C_INNER, C_INNER + C_GROUPS * C_STATE], axis=-1)
    xh = xs.reshape(bsz, nc, CHUNK, C_GROUPS, C_HEADS_PER_GROUP, C_HEAD_DIM).transpose(0, 3, 4, 1, 2, 5)
    bm = bm.reshape(bsz, nc, CHUNK, C_GROUPS, C_STATE).transpose(0, 3, 1, 2, 4)
    cm = cm.reshape(bsz, nc, CHUNK, C_GROUPS, C_STATE).transpose(0, 3, 1, 2, 4)
    dt = jax.nn.softplus(dt.astype(f32) + dt_bias)
    dt = dt.reshape(bsz, nc, CHUNK, C_GROUPS, C_HEADS_PER_GROUP).transpose(0, 3, 4, 1, 2)
    a = -jnp.exp(a_log.astype(f32)).reshape(C_GROUPS, C_HEADS_PER_GROUP)
    lc = jnp.cumsum(dt * a[None, :, :, None, None], axis=-1)
    decay = jnp.exp(-jnp.abs(lc[..., :, None] - lc[..., None, :]))
    cb = jnp.einsum('bgcin,bgcjn->bgcij', cm, bm)
    wmat = cb[:, :, None] * decay * dt[..., None, :]
    y = jnp.einsum('bghcij,bghcjp->bghcip', wmat, xh)

    def step(state, inp):
        c_c, b_c, x_c, lc_c, dt_c = inp
        y_c = jnp.einsum('bgln,bghnp->bghlp', c_c, state) * jnp.exp(lc_c)[..., None]
        wr = jnp.exp(lc_c[..., -1:] - lc_c) * dt_c
        state = (jnp.exp(lc_c[..., -1])[..., None, None] * state
                 + jnp.einsum('bgln,bghlp->bghnp', b_c, x_c * wr[..., None]))
        return state, y_c

    h0 = jnp.zeros((bsz, C_GROUPS, C_HEADS_PER_GROUP, C_STATE, C_HEAD_DIM), f32)
    _, y_inter = lax.scan(step, h0, (jnp.moveaxis(cm, 2, 0), jnp.moveaxis(bm, 2, 0), jnp.moveaxis(xh, 3, 0),
                                     jnp.moveaxis(lc, 3, 0), jnp.moveaxis(dt, 3, 0)))
    y = y + jnp.moveaxis(y_inter, 0, 3) + d_skip.reshape(C_GROUPS, C_HEADS_PER_GROUP)[None, :, :, None, None, None] * xh
    y = y.transpose(0, 3, 4, 1, 2, 5).reshape(bsz, s, C_INNER)
    y = y * jax.nn.silu(z.astype(f32))
    y = _rmsnorm(y.reshape(bsz, s, C_GROUPS, C_INNER // C_GROUPS), norm_w.reshape(C_GROUPS, -1))
    return y.reshape(bsz, s, C_INNER)


def _token_mixers(x, w_in, conv_a, a_log_a, dt_bias_a, norm_a, norm_b, conv_c, conv_bias_c,
                  dt_bias_c, a_log_c, d_skip_c, norm_c, b_gate, w_branch_a, w_branch_b,
                  w_branch_c, w_out, cos, sin):
    bsz, s, d = x.shape
    proj = x @ w_in
    split_points = np.cumsum(IN_SIZES)[:-1].tolist()
    (a_qkv, a_z, a_a, a_b, b_q, b_k, b_v, b_g, c_z, c_xbc, c_dt, gate_in) = jnp.split(proj, split_points, axis=-1)
    y_a = _gated_deltanet(a_qkv, a_z, a_a, a_b, conv_a, a_log_a, dt_bias_a, norm_a).astype(x.dtype)
    y_b = _retention(b_q, b_k, b_v, b_g, norm_b, cos, sin).astype(x.dtype)
    y_c = _ssd(c_z, c_xbc, c_dt, conv_c, conv_bias_c, dt_bias_c, a_log_c, d_skip_c, norm_c).astype(x.dtype)
    gates = jax.nn.sigmoid(gate_in.reshape(bsz, s, N_BRANCH, d) + b_gate)
    merged = (gates[:, :, 0] * (y_a @ w_branch_a) + gates[:, :, 1] * (y_b @ w_branch_b)
              + gates[:, :, 2] * (y_c @ w_branch_c))
    return merged @ w_out


def _hier_moe(x, w_rg, b_rg, w_re, b_re, w_gate, w_up, w_down):
    f32 = jnp.float32
    bsz, s, d = x.shape
    xt = x.reshape(-1, d)
    g_logits = (xt @ w_rg).astype(f32) + b_rg
    g_prob = jax.nn.softmax(g_logits, axis=-1)
    g_idx = jnp.argmax(g_logits, axis=-1)
    g_p = jnp.take_along_axis(g_prob, g_idx[:, None], axis=-1)
    e_logits = ((xt @ w_re).astype(f32) + b_re).reshape(-1, N_GROUPS, EXPERTS_PER_GROUP)
    e_sel = jnp.take_along_axis(e_logits, g_idx[:, None, None], axis=1)[:, 0]
    top_v, top_i = lax.top_k(e_sel, TOP_K_IN_GROUP)
    top_w = jax.nn.softmax(top_v, axis=-1) * g_p
    eid = g_idx[:, None] * EXPERTS_PER_GROUP + top_i
    combine = jnp.sum(jax.nn.one_hot(eid, N_EXPERTS, dtype=f32) * top_w[..., None], axis=1).astype(x.dtype)
    y = jnp.zeros_like(xt)
    for e in range(N_EXPERTS):
        h = jax.nn.silu(xt @ w_gate[e]) * (xt @ w_up[e])
        y = y + combine[:, e:e + 1] * (h @ w_down[e])
    return y.reshape(bsz, s, d)


def setup_inputs(seed: int = 0) -> dict:
    key = jax.random.key(seed)
    ks = iter(jax.random.split(key, 40))
    f32 = jnp.float32

    def nrm(shape, scale):
        return jax.random.normal(next(ks), shape, f32) * scale

    def gain(shape):
        return 1.0 + nrm(shape, 0.02)

    def dt_bias(shape):
        dt = jnp.exp(jax.random.uniform(next(ks), shape, f32, math.log(1e-3), math.log(1e-1)))
        return dt + jnp.log(-jnp.expm1(-dt))

    def a_log(shape):
        return jnp.log(jax.random.uniform(next(ks), shape, f32, 1.0, 16.0))

    L = DEPTH
    return {
        "x": nrm((BATCH, SEQ, D_MODEL), 1.0),
        "w_in": nrm((L, D_MODEL, P_IN), D_MODEL ** -0.5),
        "conv_a": nrm((L, CONV_K, 2 * A_QK_W + A_V_W), CONV_K ** -0.5),
        "a_log_a": a_log((L, A_HEADS)),
        "dt_bias_a": dt_bias((L, A_HEADS)),
        "norm_a": gain((L, A_DV)),
        "norm_b": gain((L, B_V_W)),
        "conv_c": nrm((L, CONV_K, C_XBC_W), CONV_K ** -0.5),
        "conv_bias_c": nrm((L, C_XBC_W), 0.01),
        "dt_bias_c": dt_bias((L, C_HEADS)),
        "a_log_c": a_log((L, C_HEADS)),
        "d_skip_c": gain((L, C_HEADS)),
        "norm_c": gain((L, C_INNER)),
        "b_gate": nrm((L, N_BRANCH, D_MODEL), 0.01),
        "w_branch_a": nrm((L, A_V_W, D_MODEL), A_V_W ** -0.5),
        "w_branch_b": nrm((L, B_V_W, D_MODEL), B_V_W ** -0.5),
        "w_branch_c": nrm((L, C_INNER, D_MODEL), C_INNER ** -0.5),
        "w_out": nrm((L, D_MODEL, D_MODEL), D_MODEL ** -0.5 * DEEPNORM_BETA),
        "ln1_g": gain((L, D_MODEL)),
        "ln1_b": nrm((L, D_MODEL), 0.01),
        "w_router_group": nrm((L, D_MODEL, N_GROUPS), D_MODEL ** -0.5),
        "b_router_group": nrm((L, N_GROUPS), 0.01),
        "w_router_expert": nrm((L, D_MODEL, N_EXPERTS), D_MODEL ** -0.5),
        "b_router_expert": nrm((L, N_EXPERTS), 0.01),
        "w_gate_e": nrm((L, N_EXPERTS, D_MODEL, D_EXPERT), D_MODEL ** -0.5),
        "w_up_e": nrm((L, N_EXPERTS, D_MODEL, D_EXPERT), D_MODEL ** -0.5),
        "w_down_e": nrm((L, N_EXPERTS, D_EXPERT, D_MODEL), D_EXPERT ** -0.5 * DEEPNORM_BETA),
        "ln2_g": gain((L, D_MODEL)),
        "ln2_b": nrm((L, D_MODEL), 0.01),
    }


def reference(x, w_in, conv_a, a_log_a, dt_bias_a, norm_a, norm_b, conv_c, conv_bias_c, dt_bias_c,
              a_log_c, d_skip_c, norm_c, b_gate, w_branch_a, w_branch_b, w_branch_c, w_out,
              ln1_g, ln1_b, w_router_group, b_router_group, w_router_expert, b_router_expert,
              w_gate_e, w_up_e, w_down_e, ln2_g, ln2_b):
    s = x.shape[1]
    pos = jnp.arange(s, dtype=jnp.float32)
    inv_freq = ROPE_BASE ** (-jnp.arange(0, B_DK, 2, dtype=jnp.float32) / B_DK)
    ang = pos[:, None] * inv_freq[None, :]
    cos, sin = jnp.cos(ang)[:, None, :], jnp.sin(ang)[:, None, :]
    for l in range(DEPTH):
        mix = _token_mixers(x, w_in[l], conv_a[l], a_log_a[l], dt_bias_a[l], norm_a[l], norm_b[l],
                            conv_c[l], conv_bias_c[l], dt_bias_c[l], a_log_c[l], d_skip_c[l], norm_c[l],
                            b_gate[l], w_branch_a[l], w_branch_b[l], w_branch_c[l], w_out[l], cos, sin)
        x = _layernorm(DEEPNORM_ALPHA * x + mix, ln1_g[l], ln1_b[l])
        ffn = _hier_moe(x, w_router_group[l], b_router_group[l], w_router_expert[l], b_router_expert[l],
                        w_gate_e[l], w_up_e[l], w_down_e[l])
        x = _layernorm(DEEPNORM_ALPHA * x + ffn, ln2_g[l], ln2_b[l])
    return x
```

```python
import functools
import math

import jax
import jax.numpy as jnp
from jax import lax
from jax.experimental import pallas as pl
from jax.experimental.pallas import tpu as pltpu

F32 = jnp.float32
BF16 = jnp.bfloat16
HIGHEST = lax.Precision.HIGHEST

D_MODEL = 1024
DEPTH = 2
CHUNK = 64
CONV_K = 4
EPS = 1e-6
LANES = 128
A_HEADS = 4
A_DK = 128
A_DV = 128
A_QKV_W = 3 * A_HEADS * A_DK
A_V_W = A_HEADS * A_DV
B_HEADS = 4
B_DK = 128
B_DV = 128
B_W = B_HEADS * B_DK
ROPE_BASE = 10000.0
C_INNER = 1024
C_HEAD_DIM = 64
C_HEADS = 16
C_GROUPS = 2
C_STATE = 128
C_XBC_W = C_INNER + 2 * C_GROUPS * C_STATE
C_GROUP_W = C_INNER // C_GROUPS
N_BRANCH = 3
N_GROUPS = 4
EXPERTS_PER_GROUP = 8
N_EXPERTS = 32
D_EXPERT = 512
DEEPNORM_ALPHA = (2 * DEPTH) ** 0.25

IN_SIZES = (A_QKV_W, A_V_W, A_HEADS, A_HEADS, B_W, B_W, B_W, B_W, C_INNER, C_XBC_W, C_HEADS,
            N_BRANCH * D_MODEL)
IN_OFFS = tuple(int(sum(IN_SIZES[:i])) for i in range(len(IN_SIZES) + 1))

SEQ_TILE = 256
TOKEN_TILE = 256
EXPERT_TILE = 256
VMEM_LIMIT = 56 * 1024 * 1024


def _dot(a, b):
    return jnp.dot(a, b, preferred_element_type=F32)


def _split3(a):
    hi = a.astype(BF16)
    r1 = a - hi.astype(F32)
    mid = r1.astype(BF16)
    lo = (r1 - mid.astype(F32)).astype(BF16)
    return hi, mid, lo


def _dot_sel_rhs(a, sel):
    hi, mid, lo = _split3(a)
    return _dot(hi, sel) + _dot(mid, sel) + _dot(lo, sel)


def _dot_sel_lhs(sel, a):
    hi, mid, lo = _split3(a)
    return _dot(sel, hi) + _dot(sel, mid) + _dot(sel, lo)


def _silu(x):
    return x * jax.nn.sigmoid(x)


def _softplus(x):
    return jnp.maximum(x, 0.0) + jnp.log(1.0 + jnp.exp(-jnp.abs(x)))


def _chunk_tri(ts, dtype):
    r = lax.broadcasted_iota(jnp.int32, (ts, ts), 0)
    c = lax.broadcasted_iota(jnp.int32, (ts, ts), 1)
    return jnp.where((r // CHUNK == c // CHUNK) & (c <= r), 1.0, 0.0).astype(dtype)


def _causal_conv(buf_ref, pre, conv_ref, ts):
    buf_ref[8:8 + ts, :] = pre
    acc = buf_ref[pl.ds(8 - (CONV_K - 1), ts), :] * conv_ref[0:1, :]
    for k in range(1, CONV_K):
        acc = acc + buf_ref[pl.ds(8 - (CONV_K - 1) + k, ts), :] * conv_ref[k:k + 1, :]
    buf_ref[8 - (CONV_K - 1):8, :] = buf_ref[8 + ts - (CONV_K - 1):8 + ts, :]
    return acc


def _mixer_a_kernel(x_ref, w_ref, conv_ref, alog_ref, dtb_ref, norm_ref, y_ref, buf_ref, state_ref, *, ts):
    nct = ts // CHUNK

    @pl.when(pl.program_id(1) == 0)
    def _():
        buf_ref[0:8, :] = jnp.zeros((8, A_QKV_W), F32)
        state_ref[...] = jnp.zeros_like(state_ref)

    xb = x_ref[0].astype(BF16)
    proj = _dot(xb, w_ref[...])
    qkv = _silu(_causal_conv(buf_ref, proj[:, :A_QKV_W], conv_ref, ts))
    z = proj[:, A_QKV_W:A_QKV_W + A_V_W]
    a_raw = proj[:, 2048:2048 + LANES]
    b_raw = proj[:, 2048 + LANES:2048 + 2 * LANES]
    beta = jax.nn.sigmoid(b_raw)
    g = -jnp.exp(alog_ref[...]) * _softplus(a_raw + dtb_ref[...])
    gc = _dot_sel_lhs(_chunk_tri(ts, BF16), g)
    gc3 = gc.reshape(nct, CHUNK, LANES)
    beta3 = beta.reshape(nct, CHUNK, LANES)
    gct3 = jnp.swapaxes(gc3, 1, 2)
    ri = lax.broadcasted_iota(jnp.int32, (CHUNK, CHUNK), 0)
    ci = lax.broadcasted_iota(jnp.int32, (CHUNK, CHUNK), 1)
    strict = (ci < ri)[None]
    eye = jnp.where(ri == ci, 1.0, 0.0).astype(F32)[None]

    for h in range(A_HEADS):
        q = qkv[:, h * A_DK:(h + 1) * A_DK]
        k = qkv[:, A_HEADS * A_DK + h * A_DK:A_HEADS * A_DK + (h + 1) * A_DK]
        v = qkv[:, 2 * A_HEADS * A_DK + h * A_DV:2 * A_HEADS * A_DK + (h + 1) * A_DV]
        q = q * lax.rsqrt(jnp.sum(q * q, axis=-1, keepdims=True) + EPS) * (A_DK ** -0.5)
        k = k * lax.rsqrt(jnp.sum(k * k, axis=-1, keepdims=True) + EPS)
        q3 = q.reshape(nct, CHUNK, A_DK)
        k3 = k.reshape(nct, CHUNK, A_DK)
        v3 = v.reshape(nct, CHUNK, A_DV)
        gcol = gc3[:, :, h:h + 1]
        grow = gct3[:, h:h + 1, :]
        bcol = beta3[:, :, h:h + 1]
        glast = gc3[:, CHUNK - 1:CHUNK, h:h + 1]
        decay = jnp.where(strict, jnp.exp(jnp.where(strict, gcol - grow, 0.0)), 0.0)
        kb = k3.astype(BF16)
        kk = jnp.einsum('cid,cjd->cij', kb, kb, preferred_element_type=F32)
        amat = bcol * kk * decay
        xinv = eye - amat
        pw = amat
        for _ in range(5):
            pw = jnp.einsum('cij,cjk->cik', pw, pw, precision=HIGHEST, preferred_element_type=F32)
            xinv = xinv + jnp.einsum('cij,cjk->cik', xinv, pw, precision=HIGHEST, preferred_element_type=F32)
        rhs = jnp.concatenate([v3 * bcol, k3 * (bcol * jnp.exp(gcol))], axis=-1)
        sol = jnp.einsum('cij,cjk->cik', xinv, rhs, precision=HIGHEST, preferred_element_type=F32)
        u3 = sol[:, :, :A_DV]
        w3 = sol[:, :, A_DV:]
        kend3 = k3 * jnp.exp(glast - gcol)
        dec3 = jnp.exp(glast)
        s = state_ref[h]
        outs = []
        for c in range(nct):
            delta = u3[c] - _dot(w3[c].astype(BF16), s.astype(BF16))
            s = dec3[c] * s + _dot(kend3[c].T.astype(BF16), delta.astype(BF16))
            outs.append(_dot(q3[c].astype(BF16), s.astype(BF16)))
        state_ref[h] = s
        o = jnp.concatenate(outs, axis=0)
        o = o * lax.rsqrt(jnp.mean(o * o, axis=-1, keepdims=True) + EPS) * norm_ref[...]
        o = o * _silu(z[:, h * A_DV:(h + 1) * A_DV])
        y_ref[0, :, h * A_DV:(h + 1) * A_DV] = o.astype(y_ref.dtype)


def _mixer_a(x, w, conv, alog, dtb, norm, ts):
    bsz, s, d = x.shape
    wcols = w.shape[1]
    full = lambda shape: pl.BlockSpec(shape, lambda b, j: (0,) * len(shape))
    return pl.pallas_call(
        functools.partial(_mixer_a_kernel, ts=ts),
        grid=(bsz, s // ts),
        in_specs=[pl.BlockSpec((1, ts, d), lambda b, j: (b, j, 0)),
                  full((d, wcols)), full((CONV_K, A_QKV_W)), full((1, LANES)), full((1, LANES)),
                  full((1, A_DV))],
        out_specs=pl.BlockSpec((1, ts, A_V_W), lambda b, j: (b, j, 0)),
        out_shape=jax.ShapeDtypeStruct((bsz, s, A_V_W), BF16),
        scratch_shapes=[pltpu.VMEM((8 + ts, A_QKV_W), F32), pltpu.VMEM((A_HEADS, A_DK, A_DV), F32)],
        compiler_params=pltpu.CompilerParams(dimension_semantics=("parallel", "arbitrary"),
                                             vmem_limit_bytes=VMEM_LIMIT),
        name="mixer_a",
    )(x, w, conv, alog, dtb, norm)


def _mixer_b_kernel(x_ref, w_ref, cos_ref, sin_ref, intra_ref, read_ref, write_ref, cdec_ref, norm_ref,
                    y_ref, state_ref, *, ts):
    nct = ts // CHUNK

    @pl.when(pl.program_id(1) == 0)
    def _():
        state_ref[...] = jnp.zeros_like(state_ref)

    xb = x_ref[0].astype(BF16)
    proj = _dot(xb, w_ref[...])
    cos = cos_ref[...]
    sin = sin_ref[...]
    for h in range(B_HEADS):
        q = proj[:, h * B_DK:(h + 1) * B_DK]
        k = proj[:, B_W + h * B_DK:B_W + (h + 1) * B_DK]
        v = proj[:, 2 * B_W + h * B_DV:2 * B_W + (h + 1) * B_DV]
        gate = proj[:, 3 * B_W + h * B_DV:3 * B_W + (h + 1) * B_DV]
        q = q * cos + pltpu.roll(q, B_DK // 2, 1) * sin
        k = (k * cos + pltpu.roll(k, B_DK // 2, 1) * sin) * (B_DK ** -0.5)
        q3 = q.reshape(nct, CHUNK, B_DK).astype(BF16)
        k3 = k.reshape(nct, CHUNK, B_DK)
        v3 = v.reshape(nct, CHUNK, B_DV).astype(BF16)
        scores = jnp.einsum('cid,cjd->cij', q3, k3.astype(BF16), preferred_element_type=F32) * intra_ref[h][None]
        o_intra = jnp.einsum('cij,cjd->cid', scores.astype(BF16), v3, preferred_element_type=F32)
        kw3 = k3 * write_ref[h][None]
        s = state_ref[h]
        outs = []
        for c in range(nct):
            outs.append(_dot(q3[c], s.astype(BF16)) * read_ref[h])
            s = cdec_ref[h] * s + _dot(kw3[c].T.astype(BF16), v3[c])
        state_ref[h] = s
        o = o_intra.reshape(ts, B_DV) + jnp.concatenate(outs, axis=0)
        mu = jnp.mean(o, axis=-1, keepdims=True)
        oc = o - mu
        var = jnp.mean(oc * oc, axis=-1, keepdims=True)
        o = oc * lax.rsqrt(var + EPS) * norm_ref[:, h * B_DV:(h + 1) * B_DV]
        o = o * _silu(gate)
        y_ref[0, :, h * B_DV:(h + 1) * B_DV] = o.astype(y_ref.dtype)


def _retention_tables():
    log_gamma = jnp.log1p(-jnp.exp2(-5.0 - jnp.arange(B_HEADS, dtype=F32)))
    idx = jnp.arange(CHUNK, dtype=F32)
    intra = jnp.exp(log_gamma[:, None, None] * jnp.abs(idx[:, None] - idx[None, :]))
    read = jnp.broadcast_to(jnp.exp(log_gamma[:, None] * (idx + 1.0))[:, :, None], (B_HEADS, CHUNK, B_DV))
    write = jnp.broadcast_to(jnp.exp(log_gamma[:, None] * (CHUNK - 1.0 - idx))[:, :, None], (B_HEADS, CHUNK, B_DK))
    cdec = jnp.broadcast_to(jnp.exp(log_gamma * CHUNK)[:, None, None], (B_HEADS, 1, B_DV))
    return intra, read, write, cdec


def _rope_tables(s):
    pos = jnp.arange(s, dtype=F32)
    inv_freq = ROPE_BASE ** (-jnp.arange(0, B_DK, 2, dtype=F32) / B_DK)
    ang = pos[:, None] * inv_freq[None, :]
    cos, sin = jnp.cos(ang), jnp.sin(ang)
    return jnp.concatenate([cos, cos], axis=-1), jnp.concatenate([-sin, sin], axis=-1)


def _mixer_b(x, w, cos2, sin2, tables, norm, ts):
    bsz, s, d = x.shape
    intra, read, write, cdec = tables
    full = lambda shape: pl.BlockSpec(shape, lambda b, j: (0,) * len(shape))
    return pl.pallas_call(
        functools.partial(_mixer_b_kernel, ts=ts),
        grid=(bsz, s // ts),
        in_specs=[pl.BlockSpec((1, ts, d), lambda b, j: (b, j, 0)),
                  full((d, 4 * B_W)),
                  pl.BlockSpec((ts, B_DK), lambda b, j: (j, 0)), pl.BlockSpec((ts, B_DK), lambda b, j: (j, 0)),
                  full(intra.shape), full(read.shape), full(write.shape), full(cdec.shape), full((1, B_W))],
        out_specs=pl.BlockSpec((1, ts, B_W), lambda b, j: (b, j, 0)),
        out_shape=jax.ShapeDtypeStruct((bsz, s, B_W), BF16),
        scratch_shapes=[pltpu.VMEM((B_HEADS, B_DK, B_DV), F32)],
        compiler_params=pltpu.CompilerParams(dimension_semantics=("parallel", "arbitrary"),
                                             vmem_limit_bytes=VMEM_LIMIT),
        name="mixer_b",
    )(x, w, cos2, sin2, intra, read, write, cdec, norm)


def _mixer_c_kernel(x_ref, w_ref, conv_ref, convb_ref, dtb_ref, alog_ref, dskip_ref, norm_ref, expand_ref,
                    y_ref, buf_ref, state_ref, *, ts):
    nct = ts // CHUNK

    @pl.when(pl.program_id(1) == 0)
    def _():
        buf_ref[0:8, :] = jnp.zeros((8, C_XBC_W), F32)
        state_ref[...] = jnp.zeros_like(state_ref)

    xb = x_ref[0].astype(BF16)
    proj = _dot(xb, w_ref[...])
    z = proj[:, :C_INNER]
    xbc = _silu(_causal_conv(buf_ref, proj[:, C_INNER:C_INNER + C_XBC_W], conv_ref, ts) + convb_ref[...])
    xs = xbc[:, :C_INNER]
    bm = xbc[:, C_INNER:C_INNER + C_GROUPS * C_STATE]
    cm = xbc[:, C_INNER + C_GROUPS * C_STATE:]
    dt = _softplus(proj[:, C_INNER + C_XBC_W:] + dtb_ref[...])
    lstep = dt * (-jnp.exp(alog_ref[...]))
    lc = _dot_sel_lhs(_chunk_tri(ts, BF16), lstep)
    expand = expand_ref[...]
    lce = _dot_sel_rhs(lc, expand)
    dte = _dot_sel_rhs(dt, expand)
    r = lax.broadcasted_iota(jnp.int32, (ts, C_INNER), 0)
    c = lax.broadcasted_iota(jnp.int32, (ts, C_INNER), 1)
    tiled_eye = (r % CHUNK) == (c % C_HEAD_DIM)
    rr = lax.broadcasted_iota(jnp.int32, (ts, ts), 0)
    cc = lax.broadcasted_iota(jnp.int32, (ts, ts), 1)
    chunk_ones = jnp.where(rr // CHUNK == cc // CHUNK, 1.0, 0.0).astype(BF16)
    lc_row = _dot_sel_lhs(chunk_ones, jnp.where(tiled_eye, lce, 0.0))
    dt_row = _dot_sel_lhs(chunk_ones, jnp.where(tiled_eye, dte, 0.0))
    decay = jnp.exp(-jnp.abs(lce - lc_row)) * dt_row

    xs3 = xs.reshape(nct, CHUNK, C_INNER)
    decay3 = decay.reshape(nct, CHUNK, C_INNER)
    lce3 = lce.reshape(nct, CHUNK, C_INNER)
    dte3 = dte.reshape(nct, CHUNK, C_INNER)
    lane = lax.broadcasted_iota(jnp.int32, (1, 1, LANES), 2)
    left = lane < C_HEAD_DIM
    y_groups = []
    for g in range(C_GROUPS):
        bm3 = bm[:, g * C_STATE:(g + 1) * C_STATE].reshape(nct, CHUNK, C_STATE).astype(BF16)
        cm3 = cm[:, g * C_STATE:(g + 1) * C_STATE].reshape(nct, CHUNK, C_STATE).astype(BF16)
        bm_t = jnp.concatenate([bm3] * (C_GROUP_W // CHUNK), axis=1)
        cbt = jnp.einsum('cin,cmn->cim', cm3, bm_t, preferred_element_type=F32)
        wm = cbt * decay3[:, :, g * C_GROUP_W:(g + 1) * C_GROUP_W]
        pair_out = []
        for p in range(C_GROUP_W // LANES):
            lo = g * C_GROUP_W + p * LANES
            xp = xs3[:, :, lo:lo + LANES]
            bd = jnp.concatenate([jnp.where(left, xp, 0.0), jnp.where(left, 0.0, xp)], axis=1)
            pair_out.append(jnp.einsum('cik,ckp->cip', wm[:, :, p * LANES:(p + 1) * LANES].astype(BF16),
                                       bd.astype(BF16), preferred_element_type=F32))
        y_intra = jnp.concatenate(pair_out, axis=-1)
        lg = lce3[:, :, g * C_GROUP_W:(g + 1) * C_GROUP_W]
        dg = dte3[:, :, g * C_GROUP_W:(g + 1) * C_GROUP_W]
        xg = xs3[:, :, g * C_GROUP_W:(g + 1) * C_GROUP_W]
        st = state_ref[g]
        outs = []
        for ch in range(nct):
            llast = lg[ch, CHUNK - 1:CHUNK, :]
            outs.append(_dot(cm3[ch], st.astype(BF16)) * jnp.exp(lg[ch]) + y_intra[ch])
            wr = jnp.exp(llast - lg[ch]) * dg[ch]
            st = jnp.exp(llast) * st + _dot(bm3[ch].T, (xg[ch] * wr).astype(BF16))
        state_ref[g] = st
        y_groups.append(jnp.concatenate(outs, axis=0))
    for g in range(C_GROUPS):
        sl = slice(g * C_GROUP_W, (g + 1) * C_GROUP_W)
        y = y_groups[g] + dskip_ref[:, sl] * xs[:, sl]
        y = y * _silu(z[:, sl])
        y = y * lax.rsqrt(jnp.mean(y * y, axis=-1, keepdims=True) + EPS) * norm_ref[:, sl]
        y_ref[0, :, sl] = y.astype(y_ref.dtype)


def _mixer_c(x, w, conv, convb, dtb, alog, dskip, norm, expand, ts):
    bsz, s, d = x.shape
    wcols = w.shape[1]
    full = lambda shape: pl.BlockSpec(shape, lambda b, j: (0,) * len(shape))
    return pl.pallas_call(
        functools.partial(_mixer_c_kernel, ts=ts),
        grid=(bsz, s // ts),
        in_specs=[pl.BlockSpec((1, ts, d), lambda b, j: (b, j, 0)),
                  full((d, wcols)), full((CONV_K, C_XBC_W)), full((1, C_XBC_W)), full((1, LANES)),
                  full((1, LANES)), full((1, C_INNER)), full((1, C_INNER)), full((LANES, C_INNER))],
        out_specs=pl.BlockSpec((1, ts, C_INNER), lambda b, j: (b, j, 0)),
        out_shape=jax.ShapeDtypeStruct((bsz, s, C_INNER), BF16),
        scratch_shapes=[pltpu.VMEM((8 + ts, C_XBC_W), F32), pltpu.VMEM((C_GROUPS, C_STATE, C_GROUP_W), F32)],
        compiler_params=pltpu.CompilerParams(dimension_semantics=("parallel", "arbitrary"),
                                             vmem_limit_bytes=VMEM_LIMIT),
        name="mixer_c",
    )(x, w, conv, convb, dtb, alog, dskip, norm, expand)


def _layernorm(h, g, b):
    mu = jnp.mean(h, axis=-1, keepdims=True)
    hc = h - mu
    var = jnp.mean(hc * hc, axis=-1, keepdims=True)
    return hc * lax.rsqrt(var + EPS) * g + b


def _merge_kernel(x_ref, ya_ref, yb_ref, yc_ref, wg_ref, bg_ref, wa_ref, wb_ref, wc_ref, wo_ref,
                  lng_ref, lnb_ref, wr_ref, br_ref, x1_ref, x1b_ref, ids_ref, wts_ref):
    x = x_ref[...]
    xb = x.astype(BF16)
    merged = None
    for i, (y_ref, w_ref) in enumerate(((ya_ref, wa_ref), (yb_ref, wb_ref), (yc_ref, wc_ref))):
        gate = jax.nn.sigmoid(_dot(xb, wg_ref[:, i * D_MODEL:(i + 1) * D_MODEL]) + bg_ref[i:i + 1, :])
        term = gate * _dot(y_ref[...], w_ref[...])
        merged = term if merged is None else merged + term
    mix = _dot(merged.astype(BF16), wo_ref[...])
    x1 = _layernorm(DEEPNORM_ALPHA * x + mix, lng_ref[...], lnb_ref[...])
    x1_ref[...] = x1
    x1b_ref[...] = x1.astype(BF16)

    logits = jnp.dot(x1, wr_ref[...], precision=HIGHEST, preferred_element_type=F32) + br_ref[...]
    lane = lax.broadcasted_iota(jnp.int32, logits.shape, 1)
    neg = jnp.float32(-jnp.inf)
    gmask = lane < N_GROUPS
    gl = jnp.where(gmask, logits, neg)
    gmax = jnp.max(gl, axis=-1, keepdims=True)
    gidx = jnp.min(jnp.where(gmask & (gl == gmax), lane, LANES), axis=-1, keepdims=True)
    gp = 1.0 / jnp.sum(jnp.where(gmask, jnp.exp(gl - gmax), 0.0), axis=-1, keepdims=True)
    emask = (lane >= N_GROUPS) & (lane < N_GROUPS + N_EXPERTS) & ((lane - N_GROUPS) // EXPERTS_PER_GROUP == gidx)
    el = jnp.where(emask, logits, neg)
    v1 = jnp.max(el, axis=-1, keepdims=True)
    i1 = jnp.min(jnp.where(emask & (el == v1), lane, LANES), axis=-1, keepdims=True)
    el2 = jnp.where(lane == i1, neg, el)
    v2 = jnp.max(el2, axis=-1, keepdims=True)
    i2 = jnp.min(jnp.where(emask & (lane != i1) & (el2 == v2), lane, LANES), axis=-1, keepdims=True)
    e21 = jnp.exp(v2 - v1)
    w1 = gp / (1.0 + e21)
    w2 = gp * e21 / (1.0 + e21)
    ids_ref[...] = jnp.where(lane == 0, i1 - N_GROUPS, jnp.where(lane == 1, i2 - N_GROUPS, 0))
    wts_ref[...] = jnp.where(lane == 0, w1, jnp.where(lane == 1, w2, 0.0))


def _merge(x2d, ya, yb, yc, wg, bg, wa, wb, wc, wo, lng, lnb, wr, br, tm):
    t, d = x2d.shape
    full = lambda shape: pl.BlockSpec(shape, lambda i: (0,) * len(shape))
    row = lambda w: pl.BlockSpec((tm, w), lambda i: (i, 0))
    return pl.pallas_call(
        _merge_kernel,
        grid=(t // tm,),
        in_specs=[row(d), row(A_V_W), row(B_W), row(C_INNER),
                  full(wg.shape), full(bg.shape), full(wa.shape), full(wb.shape), full(wc.shape), full(wo.shape),
                  full((1, d)), full((1, d)), full(wr.shape), full((1, LANES))],
        out_specs=[row(d), row(d), row(LANES), row(LANES)],
        out_shape=[jax.ShapeDtypeStruct((t, d), F32), jax.ShapeDtypeStruct((t, d), BF16),
                   jax.ShapeDtypeStruct((t, LANES), jnp.int32), jax.ShapeDtypeStruct((t, LANES), F32)],
        compiler_params=pltpu.CompilerParams(dimension_semantics=("parallel",), vmem_limit_bytes=VMEM_LIMIT),
        name="merge_ln_router",
    )(x2d, ya, yb, yc, wg, bg, wa, wb, wc, wo, lng, lnb, wr, br)


def _experts_kernel(te_ref, nv_ref, xs_ref, rw_ref, wg_ref, wu_ref, wd_ref, ys_ref):
    i = pl.program_id(0)

    @pl.when(i < nv_ref[0])
    def _():
        xs = xs_ref[...]
        h = _silu(_dot(xs, wg_ref[0])) * _dot(xs, wu_ref[0])
        ys_ref[...] = (rw_ref[...] * _dot(h.astype(BF16), wd_ref[0])).astype(ys_ref.dtype)

    @pl.when(i >= nv_ref[0])
    def _():
        ys_ref[...] = jnp.zeros_like(ys_ref)


def _experts(tile_expert, n_valid, xs, rw, wg, wu, wd, tm):
    rows, d = xs.shape
    grid_spec = pltpu.PrefetchScalarGridSpec(
        num_scalar_prefetch=2,
        grid=(rows // tm,),
        in_specs=[pl.BlockSpec((tm, d), lambda i, te, nv: (i, 0)),
                  pl.BlockSpec((tm, 1), lambda i, te, nv: (i, 0)),
                  pl.BlockSpec((1, d, D_EXPERT), lambda i, te, nv: (te[i], 0, 0)),
                  pl.BlockSpec((1, d, D_EXPERT), lambda i, te, nv: (te[i], 0, 0)),
                  pl.BlockSpec((1, D_EXPERT, d), lambda i, te, nv: (te[i], 0, 0))],
        out_specs=pl.BlockSpec((tm, d), lambda i, te, nv: (i, 0)),
    )
    return pl.pallas_call(
        _experts_kernel,
        grid_spec=grid_spec,
        out_shape=jax.ShapeDtypeStruct((rows, d), BF16),
        compiler_params=pltpu.CompilerParams(dimension_semantics=("arbitrary",), vmem_limit_bytes=VMEM_LIMIT),
        name="grouped_experts",
    )(tile_expert, n_valid, xs, rw, wg, wu, wd)


def _combine_kernel(x_ref, g1_ref, g2_ref, lng_ref, lnb_ref, o_ref):
    h = DEEPNORM_ALPHA * x_ref[...] + g1_ref[...].astype(F32) + g2_ref[...].astype(F32)
    o_ref[...] = _layernorm(h, lng_ref[...], lnb_ref[...])


def _combine(x1, g1, g2, lng, lnb, tm):
    t, d = x1.shape
    row = pl.BlockSpec((tm, d), lambda i: (i, 0))
    vec = pl.BlockSpec((1, d), lambda i: (0, 0))
    return pl.pallas_call(
        _combine_kernel,
        grid=(t // tm,),
        in_specs=[row, row, row, vec, vec],
        out_specs=row,
        out_shape=jax.ShapeDtypeStruct((t, d), F32),
        compiler_params=pltpu.CompilerParams(dimension_semantics=("parallel",)),
        name="combine_ln",
    )(x1, g1, g2, lng, lnb)


def _pad_lanes(v, width=LANES):
    v = v.reshape(1, -1).astype(F32)
    return jnp.pad(v, ((0, 0), (0, width - v.shape[1])))


def _dispatch(ids, wts, t, tm):
    n_pairs = 2 * t
    n_tiles = n_pairs // tm + N_EXPERTS
    flat_e = ids[:, :2].reshape(-1)
    flat_w = wts[:, :2].reshape(-1)
    order = jnp.argsort(flat_e, stable=True)
    sorted_e = flat_e[order]
    counts = jnp.sum(jax.nn.one_hot(flat_e, N_EXPERTS, dtype=jnp.int32), axis=0)
    padded = ((counts + tm - 1) // tm) * tm
    pend = jnp.cumsum(padded)
    pstart = pend - padded
    start = jnp.cumsum(counts) - counts
    dest = pstart[sorted_e] + jnp.arange(n_pairs, dtype=jnp.int32) - start[sorted_e]
    row_tok = jnp.zeros((n_tiles * tm,), jnp.int32).at[dest].set((order // 2).astype(jnp.int32))
    row_w = jnp.zeros((n_tiles * tm,), F32).at[dest].set(flat_w[order])
    pos = jnp.zeros((n_pairs,), jnp.int32).at[order].set(dest.astype(jnp.int32)).reshape(t, 2)
    n_valid = (pend[-1] // tm).astype(jnp.int32).reshape(1)
    tile_start = jnp.arange(n_tiles, dtype=jnp.int32) * tm
    tile_e = jnp.searchsorted(pend, tile_start, side='right').astype(jnp.int32)
    last_e = jnp.searchsorted(pend, pend[-1] - 1, side='right').astype(jnp.int32)
    tile_e = jnp.minimum(tile_e, last_e)
    return row_tok, row_w, pos, tile_e, n_valid


def _layer(x, p, consts, seq_tile, token_tile, expert_tile):
    bsz, s, d = x.shape
    t = bsz * s
    ya = _mixer_a(x, p["w_a"], p["conv_a"], p["a_log_a"], p["dt_bias_a"], p["norm_a"], seq_tile)
    yb = _mixer_b(x, p["w_b"], consts["cos2"], consts["sin2"], consts["ret"], p["norm_b"], seq_tile)
    yc = _mixer_c(x, p["w_c"], p["conv_c"], p["conv_bias_c"], p["dt_bias_c"], p["a_log_c"], p["d_skip_c"],
                  p["norm_c"], consts["expand"], seq_tile)
    x1, x1b, ids, wts = _merge(x.reshape(t, d), ya.reshape(t, -1), yb.reshape(t, -1), yc.reshape(t, -1),
                               p["w_gate_in"], p["b_gate"], p["w_branch_a"], p["w_branch_b"], p["w_branch_c"],
                               p["w_out"], p["ln1_g"], p["ln1_b"], p["w_router"], p["b_router"], token_tile)
    row_tok, row_w, pos, tile_e, n_valid = _dispatch(ids, wts, t, expert_tile)
    xs = jnp.take(x1b, row_tok, axis=0)
    ys = _experts(tile_e, n_valid, xs, row_w[:, None], p["w_gate_e"], p["w_up_e"], p["w_down_e"], expert_tile)
    g1 = jnp.take(ys, pos[:, 0], axis=0)
    g2 = jnp.take(ys, pos[:, 1], axis=0)
    x2 = _combine(x1, g1, g2, p["ln2_g"], p["ln2_b"], token_tile)
    return x2.reshape(bsz, s, d)


def _layer_params(l, w_in, conv_a, a_log_a, dt_bias_a, norm_a, norm_b, conv_c, conv_bias_c, dt_bias_c,
                  a_log_c, d_skip_c, norm_c, b_gate, w_branch_a, w_branch_b, w_branch_c, w_out,
                  ln1_g, ln1_b, w_router_group, b_router_group, w_router_expert, b_router_expert,
                  w_gate_e, w_up_e, w_down_e, ln2_g, ln2_b):
    o = IN_OFFS
    w = w_in[l]
    d = w.shape[0]
    zpad = lambda n: jnp.zeros((d, n), F32)
    w_a = jnp.concatenate([w[:, o[0]:o[2]], w[:, o[2]:o[3]], zpad(LANES - A_HEADS),
                           w[:, o[3]:o[4]], zpad(LANES - A_HEADS)], axis=1).astype(BF16)
    w_b = w[:, o[4]:o[8]].astype(BF16)
    w_c = jnp.concatenate([w[:, o[8]:o[10]], w[:, o[10]:o[11]], zpad(LANES - C_HEADS)], axis=1).astype(BF16)
    w_router = jnp.concatenate([w_router_group[l], w_router_expert[l],
                                zpad(LANES - N_GROUPS - N_EXPERTS)], axis=1)
    b_router = _pad_lanes(jnp.concatenate([b_router_group[l], b_router_expert[l]]))
    return {
        "w_a": w_a, "w_b": w_b, "w_c": w_c, "w_gate_in": w[:, o[11]:o[12]].astype(BF16),
        "conv_a": conv_a[l], "a_log_a": _pad_lanes(a_log_a[l]), "dt_bias_a": _pad_lanes(dt_bias_a[l]),
        "norm_a": norm_a[l].reshape(1, -1), "norm_b": norm_b[l].reshape(1, -1),
        "conv_c": conv_c[l], "conv_bias_c": conv_bias_c[l].reshape(1, -1),
        "dt_bias_c": _pad_lanes(dt_bias_c[l]), "a_log_c": _pad_lanes(a_log_c[l]),
        "d_skip_c": jnp.repeat(d_skip_c[l], C_HEAD_DIM).reshape(1, -1), "norm_c": norm_c[l].reshape(1, -1),
        "b_gate": b_gate[l],
        "w_branch_a": w_branch_a[l].astype(BF16), "w_branch_b": w_branch_b[l].astype(BF16),
        "w_branch_c": w_branch_c[l].astype(BF16), "w_out": w_out[l].astype(BF16),
        "ln1_g": ln1_g[l].reshape(1, -1), "ln1_b": ln1_b[l].reshape(1, -1),
        "w_router": w_router, "b_router": b_router,
        "w_gate_e": w_gate_e[l].astype(BF16), "w_up_e": w_up_e[l].astype(BF16),
        "w_down_e": w_down_e[l].astype(BF16),
        "ln2_g": ln2_g[l].reshape(1, -1), "ln2_b": ln2_b[l].reshape(1, -1),
    }


def _forward(x, params, seq_tile=SEQ_TILE, token_tile=TOKEN_TILE, expert_tile=EXPERT_TILE):
    s = x.shape[1]
    cos2, sin2 = _rope_tables(s)
    head_of_lane = jnp.arange(C_INNER) // C_HEAD_DIM
    expand = (jnp.arange(LANES)[:, None] == head_of_lane[None, :]).astype(BF16)
    consts = {"cos2": cos2, "sin2": sin2, "ret": _retention_tables(), "expand": expand}
    for l in range(DEPTH):
        x = _layer(x, _layer_params(l, *params), consts, seq_tile, token_tile, expert_tile)
    return x


def kernel(x, w_in, conv_a, a_log_a, dt_bias_a, norm_a, norm_b, conv_c, conv_bias_c, dt_bias_c, a_log_c, d_skip_c, norm_c, b_gate, w_branch_a, w_branch_b, w_branch_c, w_out, ln1_g, ln1_b, w_router_group, b_router_group, w_router_expert, b_router_expert, w_gate_e, w_up_e, w_down_e, ln2_g, ln2_b):
    params = (w_in, conv_a, a_log_a, dt_bias_a, norm_a, norm_b, conv_c, conv_bias_c, dt_bias_c, a_log_c,
              d_skip_c, norm_c, b_gate, w_branch_a, w_branch_b, w_branch_c, w_out, ln1_g, ln1_b,
              w_router_group, b_router_group, w_router_expert, b_router_expert, w_gate_e, w_up_e, w_down_e,
              ln2_g, ln2_b)
    return _forward(x, params)
```

```python
import functools
import math

import jax
import jax.numpy as jnp
from jax import lax
from jax.experimental import pallas as pl
from jax.experimental.pallas import tpu as pltpu

F32 = jnp.float32
BF16 = jnp.bfloat16
HIGHEST = lax.Precision.HIGHEST

D_MODEL = 1024
DEPTH = 2
CHUNK = 64
CONV_K = 4
EPS = 1e-6
LANES = 128
A_HEADS = 4
A_DK = 128
A_DV = 128
A_QKV_W = 3 * A_HEADS * A_DK
A_V_W = A_HEADS * A_DV
B_HEADS = 4
B_DK = 128
B_DV = 128
B_W = B_HEADS * B_DK
ROPE_BASE = 10000.0
C_INNER = 1024
C_HEAD_DIM = 64
C_HEADS = 16
C_GROUPS = 2
C_STATE = 128
C_XBC_W = C_INNER + 2 * C_GROUPS * C_STATE
C_GROUP_W = C_INNER // C_GROUPS
N_BRANCH = 3
N_GROUPS = 4
EXPERTS_PER_GROUP = 8
N_EXPERTS = 32
D_EXPERT = 512
DEEPNORM_ALPHA = (2 * DEPTH) ** 0.25

IN_SIZES = (A_QKV_W, A_V_W, A_HEADS, A_HEADS, B_W, B_W, B_W, B_W, C_INNER, C_XBC_W, C_HEADS,
            N_BRANCH * D_MODEL)
IN_OFFS = tuple(int(sum(IN_SIZES[:i])) for i in range(len(IN_SIZES) + 1))

SEQ_TILE = 256
TOKEN_TILE = 512
EXPERT_TILE = 256
VMEM_LIMIT = 56 * 1024 * 1024


def _dot(a, b):
    return jnp.dot(a, b, preferred_element_type=F32)


def _split3(a):
    hi = a.astype(BF16)
    r1 = a - hi.astype(F32)
    mid = r1.astype(BF16)
    lo = (r1 - mid.astype(F32)).astype(BF16)
    return hi, mid, lo


def _split2(a):
    hi = a.astype(BF16)
    return hi, (a - hi.astype(F32)).astype(BF16)


def _bmm(a, b):
    return jnp.einsum('cij,cjk->cik', a, b, preferred_element_type=F32)


def _dot_sel_rhs(a, sel):
    hi, mid, lo = _split3(a)
    return _dot(hi, sel) + _dot(mid, sel) + _dot(lo, sel)


def _dot_sel_lhs(sel, a):
    hi, mid, lo = _split3(a)
    return _dot(sel, hi) + _dot(sel, mid) + _dot(sel, lo)


def _bmm_sel_lhs(sel, a):
    hi, mid, lo = _split3(a)
    return _bmm(sel, hi) + _bmm(sel, mid) + _bmm(sel, lo)


def _silu(x):
    return x * jax.nn.sigmoid(x)


def _softplus(x):
    return jnp.maximum(x, 0.0) + jnp.log(1.0 + jnp.exp(-jnp.abs(x)))


def _chunk_tri(ts, dtype):
    r = lax.broadcasted_iota(jnp.int32, (ts, ts), 0)
    c = lax.broadcasted_iota(jnp.int32, (ts, ts), 1)
    return jnp.where((r // CHUNK == c // CHUNK) & (c <= r), 1.0, 0.0).astype(dtype)


def _causal_conv(buf_ref, pre, conv_ref, ts):
    buf_ref[8:8 + ts, :] = pre
    acc = buf_ref[pl.ds(8 - (CONV_K - 1), ts), :] * conv_ref[0:1, :]
    for k in range(1, CONV_K):
        acc = acc + buf_ref[pl.ds(8 - (CONV_K - 1) + k, ts), :] * conv_ref[k:k + 1, :]
    buf_ref[8 - (CONV_K - 1):8, :] = buf_ref[8 + ts - (CONV_K - 1):8 + ts, :]
    return acc


def _mixer_a_kernel(x_ref, w_ref, conv_ref, alog_ref, dtb_ref, norm_ref, y_ref, buf_ref, state_ref, *, ts):
    nct = ts // CHUNK

    @pl.when(pl.program_id(1) == 0)
    def _():
        buf_ref[0:8, :] = jnp.zeros((8, A_QKV_W), F32)
        state_ref[...] = jnp.zeros_like(state_ref)

    xb = x_ref[0].astype(BF16)
    proj = _dot(xb, w_ref[...])
    qkv = _silu(_causal_conv(buf_ref, proj[:, :A_QKV_W], conv_ref, ts))
    z = proj[:, A_QKV_W:A_QKV_W + A_V_W]
    a_raw = proj[:, 2048:2048 + LANES]
    b_raw = proj[:, 2048 + LANES:2048 + 2 * LANES]
    beta = jax.nn.sigmoid(b_raw)
    g = -jnp.exp(alog_ref[...]) * _softplus(a_raw + dtb_ref[...])
    gc = _dot_sel_lhs(_chunk_tri(ts, BF16), g)
    gc3 = gc.reshape(nct, CHUNK, LANES)
    beta3 = beta.reshape(nct, CHUNK, LANES)
    gct3 = jnp.swapaxes(gc3, 1, 2)
    ri = lax.broadcasted_iota(jnp.int32, (CHUNK, CHUNK), 0)
    ci = lax.broadcasted_iota(jnp.int32, (CHUNK, CHUNK), 1)
    strict = (ci < ri)[None]
    eye = jnp.where(ri == ci, 1.0, 0.0).astype(F32)[None]
    nh = A_HEADS

    def stack(per_head):
        parts = [per_head(h) for h in range(nh)]
        return jnp.concatenate([parts[h][c:c + 1] for c in range(nct) for h in range(nh)], axis=0)

    q = stack(lambda h: qkv[:, h * A_DK:(h + 1) * A_DK].reshape(nct, CHUNK, A_DK))
    k = stack(lambda h: qkv[:, (nh + h) * A_DK:(nh + h + 1) * A_DK].reshape(nct, CHUNK, A_DK))
    v = stack(lambda h: qkv[:, 2 * nh * A_DK + h * A_DV:2 * nh * A_DK + (h + 1) * A_DV].reshape(nct, CHUNK, A_DV))
    zs = stack(lambda h: z[:, h * A_DV:(h + 1) * A_DV].reshape(nct, CHUNK, A_DV))
    gcol = stack(lambda h: gc3[:, :, h:h + 1])
    grow = stack(lambda h: gct3[:, h:h + 1, :])
    bcol = stack(lambda h: beta3[:, :, h:h + 1])
    glast = stack(lambda h: gc3[:, CHUNK - 1:CHUNK, h:h + 1])
    q = q * lax.rsqrt(jnp.sum(q * q, axis=-1, keepdims=True) + EPS) * (A_DK ** -0.5)
    k = k * lax.rsqrt(jnp.sum(k * k, axis=-1, keepdims=True) + EPS)
    decay = jnp.where(strict, jnp.exp(jnp.where(strict, gcol - grow, 0.0)), 0.0)
    kb = k.astype(BF16)
    amat = bcol * jnp.einsum('bid,bjd->bij', kb, kb, preferred_element_type=F32) * decay
    xinv = eye - amat
    pw = amat.astype(BF16)
    for _ in range(5):
        pw = _bmm(pw, pw).astype(BF16)
        xinv = xinv + _bmm(xinv.astype(BF16), pw)
    ah, al = _split2(amat)
    xh, xl = _split2(xinv)
    resid = eye - xinv - (_bmm(ah, xh) + _bmm(ah, xl) + _bmm(al, xh))
    xinv = xinv + _bmm(xh, resid.astype(BF16))
    rhs = jnp.concatenate([v * bcol, k * (bcol * jnp.exp(gcol))], axis=-1)
    sol = _bmm(xinv.astype(BF16), rhs.astype(BF16))
    u = sol[:, :, :A_DV]
    wq = jnp.concatenate([sol[:, :, A_DV:], q], axis=1).astype(BF16)
    kend = (k * jnp.exp(glast - gcol)).astype(BF16)
    qk = jnp.einsum('bid,bjd->bij', q.astype(BF16), kend, preferred_element_type=F32).astype(BF16)
    kend_t = jnp.swapaxes(kend, 1, 2)
    dec = jnp.exp(glast)
    s = state_ref[...]
    outs = []
    for c in range(nct):
        sl = slice(c * nh, (c + 1) * nh)
        r = _bmm(wq[sl], s.astype(BF16))
        delta = (u[sl] - r[:, :CHUNK]).astype(BF16)
        outs.append(dec[sl] * r[:, CHUNK:] + _bmm(qk[sl], delta))
        s = dec[sl] * s + _bmm(kend_t[sl], delta)
    state_ref[...] = s
    o = jnp.concatenate(outs, axis=0)
    o = o * lax.rsqrt(jnp.mean(o * o, axis=-1, keepdims=True) + EPS) * norm_ref[...]
    o = (o * _silu(zs)).astype(y_ref.dtype)
    for c in range(nct):
        for h in range(nh):
            y_ref[0, c * CHUNK:(c + 1) * CHUNK, h * A_DV:(h + 1) * A_DV] = o[c * nh + h]


def _mixer_a(x, w, conv, alog, dtb, norm, ts):
    bsz, s, d = x.shape
    wcols = w.shape[1]
    full = lambda shape: pl.BlockSpec(shape, lambda b, j: (0,) * len(shape))
    return pl.pallas_call(
        functools.partial(_mixer_a_kernel, ts=ts),
        grid=(bsz, s // ts),
        in_specs=[pl.BlockSpec((1, ts, d), lambda b, j: (b, j, 0)),
                  full((d, wcols)), full((CONV_K, A_QKV_W)), full((1, LANES)), full((1, LANES)),
                  full((1, A_DV))],
        out_specs=pl.BlockSpec((1, ts, A_V_W), lambda b, j: (b, j, 0)),
        out_shape=jax.ShapeDtypeStruct((bsz, s, A_V_W), BF16),
        scratch_shapes=[pltpu.VMEM((8 + ts, A_QKV_W), F32), pltpu.VMEM((A_HEADS, A_DK, A_DV), F32)],
        compiler_params=pltpu.CompilerParams(dimension_semantics=("parallel", "arbitrary"),
                                             vmem_limit_bytes=VMEM_LIMIT),
        name="mixer_a",
    )(x, w, conv, alog, dtb, norm)


def _mixer_b_kernel(x_ref, w_ref, cos_ref, sin_ref, intra_ref, read_ref, write_ref, cdec_ref, norm_ref,
                    y_ref, state_ref, *, ts):
    nct = ts // CHUNK

    @pl.when(pl.program_id(1) == 0)
    def _():
        state_ref[...] = jnp.zeros_like(state_ref)

    xb = x_ref[0].astype(BF16)
    proj = _dot(xb, w_ref[...])
    cos = cos_ref[...]
    sin = sin_ref[...]
    nh = B_HEADS

    def stack(per_head):
        parts = [per_head(h) for h in range(nh)]
        return jnp.concatenate([parts[h][c:c + 1] for c in range(nct) for h in range(nh)], axis=0)

    def rope(t):
        return t * cos + pltpu.roll(t, B_DK // 2, 1) * sin

    q = stack(lambda h: rope(proj[:, h * B_DK:(h + 1) * B_DK]).reshape(nct, CHUNK, B_DK)).astype(BF16)
    k = stack(lambda h: (rope(proj[:, B_W + h * B_DK:B_W + (h + 1) * B_DK]) * (B_DK ** -0.5))
              .reshape(nct, CHUNK, B_DK))
    v = stack(lambda h: proj[:, 2 * B_W + h * B_DV:2 * B_W + (h + 1) * B_DV].reshape(nct, CHUNK, B_DV)).astype(BF16)
    gate = stack(lambda h: proj[:, 3 * B_W + h * B_DV:3 * B_W + (h + 1) * B_DV].reshape(nct, CHUNK, B_DV))
    tile_heads = lambda ref: jnp.concatenate([ref[...]] * nct, axis=0)
    scores = jnp.einsum('bid,bjd->bij', q, k.astype(BF16), preferred_element_type=F32) * tile_heads(intra_ref)
    o = _bmm(scores.astype(BF16), v)
    kw_t = jnp.swapaxes((k * tile_heads(write_ref)).astype(BF16), 1, 2)
    kv = _bmm(kw_t, v)
    s = state_ref[...]
    cdec = cdec_ref[...]
    prev = []
    for c in range(nct):
        prev.append(s.astype(BF16))
        s = cdec * s + kv[c * nh:(c + 1) * nh]
    state_ref[...] = s
    o = o + _bmm(q, jnp.concatenate(prev, axis=0)) * tile_heads(read_ref)
    mu = jnp.mean(o, axis=-1, keepdims=True)
    oc = o - mu
    var = jnp.mean(oc * oc, axis=-1, keepdims=True)
    norm = jnp.concatenate([norm_ref[:, h * B_DV:(h + 1) * B_DV][None] for h in range(nh)] * nct, axis=0)
    o = (oc * lax.rsqrt(var + EPS) * norm * _silu(gate)).astype(y_ref.dtype)
    for c in range(nct):
        for h in range(nh):
            y_ref[0, c * CHUNK:(c + 1) * CHUNK, h * B_DV:(h + 1) * B_DV] = o[c * nh + h]


def _retention_tables():
    log_gamma = jnp.log1p(-jnp.exp2(-5.0 - jnp.arange(B_HEADS, dtype=F32)))
    idx = jnp.arange(CHUNK, dtype=F32)
    intra = jnp.exp(log_gamma[:, None, None] * jnp.abs(idx[:, None] - idx[None, :]))
    read = jnp.broadcast_to(jnp.exp(log_gamma[:, None] * (idx + 1.0))[:, :, None], (B_HEADS, CHUNK, B_DV))
    write = jnp.broadcast_to(jnp.exp(log_gamma[:, None] * (CHUNK - 1.0 - idx))[:, :, None], (B_HEADS, CHUNK, B_DK))
    cdec = jnp.broadcast_to(jnp.exp(log_gamma * CHUNK)[:, None, None], (B_HEADS, 1, B_DV))
    return intra, read, write, cdec


def _rope_tables(s):
    pos = jnp.arange(s, dtype=F32)
    inv_freq = ROPE_BASE ** (-jnp.arange(0, B_DK, 2, dtype=F32) / B_DK)
    ang = pos[:, None] * inv_freq[None, :]
    cos, sin = jnp.cos(ang), jnp.sin(ang)
    return jnp.concatenate([cos, cos], axis=-1), jnp.concatenate([-sin, sin], axis=-1)


def _mixer_b(x, w, cos2, sin2, tables, norm, ts):
    bsz, s, d = x.shape
    intra, read, write, cdec = tables
    full = lambda shape: pl.BlockSpec(shape, lambda b, j: (0,) * len(shape))
    return pl.pallas_call(
        functools.partial(_mixer_b_kernel, ts=ts),
        grid=(bsz, s // ts),
        in_specs=[pl.BlockSpec((1, ts, d), lambda b, j: (b, j, 0)),
                  full((d, 4 * B_W)),
                  pl.BlockSpec((ts, B_DK), lambda b, j: (j, 0)), pl.BlockSpec((ts, B_DK), lambda b, j: (j, 0)),
                  full(intra.shape), full(read.shape), full(write.shape), full(cdec.shape), full((1, B_W))],
        out_specs=pl.BlockSpec((1, ts, B_W), lambda b, j: (b, j, 0)),
        out_shape=jax.ShapeDtypeStruct((bsz, s, B_W), BF16),
        scratch_shapes=[pltpu.VMEM((B_HEADS, B_DK, B_DV), F32)],
        compiler_params=pltpu.CompilerParams(dimension_semantics=("parallel", "arbitrary"),
                                             vmem_limit_bytes=VMEM_LIMIT),
        name="mixer_b",
    )(x, w, cos2, sin2, intra, read, write, cdec, norm)


def _mixer_c_kernel(x_ref, w_ref, conv_ref, convb_ref, dtb_ref, alog_ref, dskip_ref, norm_ref, expand_ref,
                    y_ref, buf_ref, state_ref, *, ts):
    nct = ts // CHUNK

    @pl.when(pl.program_id(1) == 0)
    def _():
        buf_ref[0:8, :] = jnp.zeros((8, C_XBC_W), F32)
        state_ref[...] = jnp.zeros_like(state_ref)

    xb = x_ref[0].astype(BF16)
    proj = _dot(xb, w_ref[...])
    z = proj[:, :C_INNER]
    xbc = _silu(_causal_conv(buf_ref, proj[:, C_INNER:C_INNER + C_XBC_W], conv_ref, ts) + convb_ref[...])
    xs = xbc[:, :C_INNER]
    bm = xbc[:, C_INNER:C_INNER + C_GROUPS * C_STATE]
    cm = xbc[:, C_INNER + C_GROUPS * C_STATE:]
    dt = _softplus(proj[:, C_INNER + C_XBC_W:] + dtb_ref[...])
    lstep = dt * (-jnp.exp(alog_ref[...]))
    lc = _dot_sel_lhs(_chunk_tri(ts, BF16), lstep)
    expand = expand_ref[...]
    lce = _dot_sel_rhs(lc, expand)
    dte = _dot_sel_rhs(dt, expand)
    ng = C_GROUPS

    def stack(per_group):
        parts = [per_group(g) for g in range(ng)]
        return jnp.concatenate([parts[g][c:c + 1] for c in range(nct) for g in range(ng)], axis=0)

    def group_lanes(a, width=C_GROUP_W):
        return stack(lambda g: a[:, g * width:(g + 1) * width].reshape(nct, CHUNK, width))

    def group_rows(ref):
        return jnp.concatenate([ref[:, g * C_GROUP_W:(g + 1) * C_GROUP_W][None] for g in range(ng)] * nct, axis=0)

    lg = group_lanes(lce)
    dg = group_lanes(dte)
    xg = group_lanes(xs)
    zg = group_lanes(z)
    bmg = group_lanes(bm, C_STATE).astype(BF16)
    cmg = group_lanes(cm, C_STATE).astype(BF16)
    r = lax.broadcasted_iota(jnp.int32, (CHUNK, C_GROUP_W), 0)
    c = lax.broadcasted_iota(jnp.int32, (CHUNK, C_GROUP_W), 1)
    tiled_eye = (r == c % C_HEAD_DIM)[None]
    ones_b = jnp.ones((ng * nct, CHUNK, CHUNK), BF16)
    lc_row = _bmm_sel_lhs(ones_b, jnp.where(tiled_eye, lg, 0.0))
    dt_row = _bmm_sel_lhs(ones_b, jnp.where(tiled_eye, dg, 0.0))
    bm_t = jnp.concatenate([bmg] * (C_GROUP_W // CHUNK), axis=1)
    cbt = jnp.einsum('bin,bmn->bim', cmg, bm_t, preferred_element_type=F32)
    wm = (cbt * jnp.exp(-jnp.abs(lg - lc_row)) * dt_row).astype(BF16)
    lane = lax.broadcasted_iota(jnp.int32, (1, 1, LANES), 2)
    left = lane < C_HEAD_DIM
    pair_out = []
    for p in range(C_GROUP_W // LANES):
        xp = xg[:, :, p * LANES:(p + 1) * LANES]
        bd = jnp.concatenate([jnp.where(left, xp, 0.0), jnp.where(left, 0.0, xp)], axis=1)
        pair_out.append(_bmm(wm[:, :, p * LANES:(p + 1) * LANES], bd.astype(BF16)))
    y = jnp.concatenate(pair_out, axis=-1)
    llast = lg[:, CHUNK - 1:CHUNK, :]
    xw = (xg * (jnp.exp(llast - lg) * dg)).astype(BF16)
    kv = _bmm(jnp.swapaxes(bmg, 1, 2), xw)
    sdec = jnp.exp(llast)
    st = state_ref[...]
    prev = []
    for ch in range(nct):
        sl = slice(ch * ng, (ch + 1) * ng)
        prev.append(st.astype(BF16))
        st = sdec[sl] * st + kv[sl]
    state_ref[...] = st
    y = y + _bmm(cmg, jnp.concatenate(prev, axis=0)) * jnp.exp(lg)
    y = (y + group_rows(dskip_ref) * xg) * _silu(zg)
    y = (y * lax.rsqrt(jnp.mean(y * y, axis=-1, keepdims=True) + EPS) * group_rows(norm_ref)).astype(y_ref.dtype)
    for ch in range(nct):
        for g in range(ng):
            y_ref[0, ch * CHUNK:(ch + 1) * CHUNK, g * C_GROUP_W:(g + 1) * C_GROUP_W] = y[ch * ng + g]


def _mixer_c(x, w, conv, convb, dtb, alog, dskip, norm, expand, ts):
    bsz, s, d = x.shape
    wcols = w.shape[1]
    full = lambda shape: pl.BlockSpec(shape, lambda b, j: (0,) * len(shape))
    return pl.pallas_call(
        functools.partial(_mixer_c_kernel, ts=ts),
        grid=(bsz, s // ts),
        in_specs=[pl.BlockSpec((1, ts, d), lambda b, j: (b, j, 0)),
                  full((d, wcols)), full((CONV_K, C_XBC_W)), full((1, C_XBC_W)), full((1, LANES)),
                  full((1, LANES)), full((1, C_INNER)), full((1, C_INNER)), full((LANES, C_INNER))],
        out_specs=pl.BlockSpec((1, ts, C_INNER), lambda b, j: (b, j, 0)),
        out_shape=jax.ShapeDtypeStruct((bsz, s, C_INNER), BF16),
        scratch_shapes=[pltpu.VMEM((8 + ts, C_XBC_W), F32), pltpu.VMEM((C_GROUPS, C_STATE, C_GROUP_W), F32)],
        compiler_params=pltpu.CompilerParams(dimension_semantics=("parallel", "arbitrary"),
                                             vmem_limit_bytes=VMEM_LIMIT),
        name="mixer_c",
    )(x, w, conv, convb, dtb, alog, dskip, norm, expand)


def _layernorm(h, g, b):
    mu = jnp.mean(h, axis=-1, keepdims=True)
    hc = h - mu
    var = jnp.mean(hc * hc, axis=-1, keepdims=True)
    return hc * lax.rsqrt(var + EPS) * g + b


def _merge_kernel(x_ref, ya_ref, yb_ref, yc_ref, wg_ref, bg_ref, wa_ref, wb_ref, wc_ref, wo_ref,
                  lng_ref, lnb_ref, wrh_ref, wrl_ref, br_ref, x1_ref, x1b_ref, ids_ref, wts_ref, cnt_ref,
                  run_ref):
    @pl.when(pl.program_id(0) == 0)
    def _():
        run_ref[...] = jnp.zeros_like(run_ref)

    x = x_ref[...]
    xb = x.astype(BF16)
    merged = None
    for i, (y_ref, w_ref) in enumerate(((ya_ref, wa_ref), (yb_ref, wb_ref), (yc_ref, wc_ref))):
        gate = jax.nn.sigmoid(_dot(xb, wg_ref[:, i * D_MODEL:(i + 1) * D_MODEL]) + bg_ref[i:i + 1, :])
        term = gate * _dot(y_ref[...], w_ref[...])
        merged = term if merged is None else merged + term
    mix = _dot(merged.astype(BF16), wo_ref[...])
    x1 = _layernorm(DEEPNORM_ALPHA * x + mix, lng_ref[...], lnb_ref[...])
    x1_ref[...] = x1
    x1b_ref[...] = x1.astype(BF16)

    xh, xl = _split2(x1)
    logits = _dot(xh, wrh_ref[...]) + _dot(xl, wrh_ref[...]) + _dot(xh, wrl_ref[...]) + br_ref[...]
    tm = logits.shape[0]
    lane = lax.broadcasted_iota(jnp.int32, logits.shape, 1)
    neg = jnp.float32(-jnp.inf)
    gmask = lane < N_GROUPS
    gl = jnp.where(gmask, logits, neg)
    gmax = jnp.max(gl, axis=-1, keepdims=True)
    gidx = jnp.min(jnp.where(gmask & (gl == gmax), lane, LANES), axis=-1, keepdims=True)
    gp = 1.0 / jnp.sum(jnp.where(gmask, jnp.exp(gl - gmax), 0.0), axis=-1, keepdims=True)
    emask = (lane >= N_GROUPS) & (lane < N_GROUPS + N_EXPERTS) & ((lane - N_GROUPS) // EXPERTS_PER_GROUP == gidx)
    el = jnp.where(emask, logits, neg)
    v1 = jnp.max(el, axis=-1, keepdims=True)
    i1 = jnp.min(jnp.where(emask & (el == v1), lane, LANES), axis=-1, keepdims=True)
    el2 = jnp.where(lane == i1, neg, el)
    v2 = jnp.max(el2, axis=-1, keepdims=True)
    i2 = jnp.min(jnp.where(emask & (lane != i1) & (el2 == v2), lane, LANES), axis=-1, keepdims=True)
    e21 = jnp.exp(v2 - v1)
    w1 = gp / (1.0 + e21)
    w2 = gp * e21 / (1.0 + e21)

    hot1 = lane == i1
    hot2 = lane == i2
    hot = jnp.where(hot1 | hot2, 1.0, 0.0)
    r = lax.broadcasted_iota(jnp.int32, (tm, tm), 0)
    c = lax.broadcasted_iota(jnp.int32, (tm, tm), 1)
    before = _dot(jnp.where(c < r, 1.0, 0.0).astype(BF16), hot.astype(BF16)) + run_ref[0:1, :]
    rank1 = jnp.sum(jnp.where(hot1, before, 0.0), axis=-1, keepdims=True).astype(jnp.int32)
    rank2 = jnp.sum(jnp.where(hot2, before, 0.0), axis=-1, keepdims=True).astype(jnp.int32)
    run_ref[0:1, :] = run_ref[0:1, :] + jnp.sum(hot, axis=0, keepdims=True)
    cnt_ref[...] = jnp.broadcast_to(run_ref[0:1, :], cnt_ref.shape)

    ids_ref[...] = jnp.where(lane == 0, i1 - N_GROUPS, jnp.where(lane == 1, i2 - N_GROUPS,
                             jnp.where(lane == 2, rank1, jnp.where(lane == 3, rank2, 0))))
    wts_ref[...] = jnp.where(lane == 0, w1, jnp.where(lane == 1, w2, 0.0))


def _merge(x2d, ya, yb, yc, wg, bg, wa, wb, wc, wo, lng, lnb, wrh, wrl, br, tm):
    t, d = x2d.shape
    full = lambda shape: pl.BlockSpec(shape, lambda i: (0,) * len(shape))
    row = lambda w: pl.BlockSpec((tm, w), lambda i: (i, 0))
    return pl.pallas_call(
        _merge_kernel,
        grid=(t // tm,),
        in_specs=[row(d), row(A_V_W), row(B_W), row(C_INNER),
                  full(wg.shape), full(bg.shape), full(wa.shape), full(wb.shape), full(wc.shape), full(wo.shape),
                  full((1, d)), full((1, d)), full(wrh.shape), full(wrl.shape), full((1, LANES))],
        out_specs=[row(d), row(d), row(LANES), row(LANES), full((8, LANES))],
        out_shape=[jax.ShapeDtypeStruct((t, d), F32), jax.ShapeDtypeStruct((t, d), BF16),
                   jax.ShapeDtypeStruct((t, LANES), jnp.int32), jax.ShapeDtypeStruct((t, LANES), F32),
                   jax.ShapeDtypeStruct((8, LANES), F32)],
        scratch_shapes=[pltpu.VMEM((8, LANES), F32)],
        compiler_params=pltpu.CompilerParams(dimension_semantics=("arbitrary",), vmem_limit_bytes=VMEM_LIMIT),
        name="merge_ln_router",
    )(x2d, ya, yb, yc, wg, bg, wa, wb, wc, wo, lng, lnb, wrh, wrl, br)


def _experts_kernel(te_ref, nv_ref, xs_ref, wg_ref, wu_ref, wd_ref, ys_ref):
    i = pl.program_id(0)

    @pl.when(i < nv_ref[0])
    def _():
        xs = xs_ref[...]
        h = _silu(_dot(xs, wg_ref[0])) * _dot(xs, wu_ref[0])
        ys_ref[...] = _dot(h.astype(BF16), wd_ref[0]).astype(ys_ref.dtype)

    @pl.when(i >= nv_ref[0])
    def _():
        ys_ref[...] = jnp.zeros_like(ys_ref)


def _experts(tile_expert, n_valid, xs, wg, wu, wd, tm):
    rows, d = xs.shape
    grid_spec = pltpu.PrefetchScalarGridSpec(
        num_scalar_prefetch=2,
        grid=(rows // tm,),
        in_specs=[pl.BlockSpec((tm, d), lambda i, te, nv: (i, 0)),
                  pl.BlockSpec((1, d, D_EXPERT), lambda i, te, nv: (te[i], 0, 0)),
                  pl.BlockSpec((1, d, D_EXPERT), lambda i, te, nv: (te[i], 0, 0)),
                  pl.BlockSpec((1, D_EXPERT, d), lambda i, te, nv: (te[i], 0, 0))],
        out_specs=pl.BlockSpec((tm, d), lambda i, te, nv: (i, 0)),
    )
    return pl.pallas_call(
        _experts_kernel,
        grid_spec=grid_spec,
        out_shape=jax.ShapeDtypeStruct((rows, d), BF16),
        compiler_params=pltpu.CompilerParams(dimension_semantics=("arbitrary",), vmem_limit_bytes=VMEM_LIMIT),
        name="grouped_experts",
    )(tile_expert, n_valid, xs, wg, wu, wd)


def _combine_kernel(x_ref, g_ref, wts_ref, lng_ref, lnb_ref, o_ref):
    d = x_ref.shape[1]
    wts = wts_ref[...]
    ffn = wts[:, 0:1] * g_ref[:, :d].astype(F32) + wts[:, 1:2] * g_ref[:, d:].astype(F32)
    o_ref[...] = _layernorm(DEEPNORM_ALPHA * x_ref[...] + ffn, lng_ref[...], lnb_ref[...])


def _combine(x1, g, wts, lng, lnb, tm):
    t, d = x1.shape
    row = lambda w: pl.BlockSpec((tm, w), lambda i: (i, 0))
    vec = pl.BlockSpec((1, d), lambda i: (0, 0))
    return pl.pallas_call(
        _combine_kernel,
        grid=(t // tm,),
        in_specs=[row(d), row(2 * d), row(LANES), vec, vec],
        out_specs=row(d),
        out_shape=jax.ShapeDtypeStruct((t, d), F32),
        compiler_params=pltpu.CompilerParams(dimension_semantics=("parallel",), vmem_limit_bytes=VMEM_LIMIT),
        name="combine_ln",
    )(x1, g, wts, lng, lnb)


def _pad_lanes(v, width=LANES):
    v = v.reshape(1, -1).astype(F32)
    return jnp.pad(v, ((0, 0), (0, width - v.shape[1])))


def _rows(a, idx):
    return a.at[idx].get(mode="promise_in_bounds")


def _dispatch(ids, counts, t, tm):
    n_pairs = 2 * t
    n_tiles = n_pairs // tm + N_EXPERTS
    e = ids[:, 0:2]
    rank = ids[:, 2:4]
    padded = ((counts + tm - 1) // tm) * tm
    pend = jnp.cumsum(padded)
    pstart = pend - padded
    start = jnp.cumsum(counts) - counts
    pos = _rows(pstart, e) + rank
    key = e.reshape(-1) * n_pairs + jnp.arange(n_pairs, dtype=jnp.int32)
    order = lax.sort(key) % n_pairs
    n_valid = (pend[-1] // tm).astype(jnp.int32).reshape(1)
    tile_start = jnp.arange(n_tiles, dtype=jnp.int32) * tm
    tile_e = jnp.searchsorted(pend, tile_start, side='right').astype(jnp.int32)
    last_e = jnp.searchsorted(pend, pend[-1] - 1, side='right').astype(jnp.int32)
    tile_e = jnp.minimum(tile_e, last_e)
    row = jnp.arange(n_tiles * tm, dtype=jnp.int32)
    row_e = jnp.repeat(tile_e, tm)
    off = row - _rows(pstart, row_e)
    valid = (off >= 0) & (off < _rows(counts, row_e))
    idx = jnp.where(valid, _rows(start, row_e) + off, 0)
    row_tok = jnp.where(valid, _rows(order, idx) // 2, 0)
    return row_tok, pos, tile_e, n_valid


def _layer(x, p, consts, seq_tile, token_tile, expert_tile):
    bsz, s, d = x.shape
    t = bsz * s
    ya = _mixer_a(x, p["w_a"], p["conv_a"], p["a_log_a"], p["dt_bias_a"], p["norm_a"], seq_tile)
    yb = _mixer_b(x, p["w_b"], consts["cos2"], consts["sin2"], consts["ret"], p["norm_b"], seq_tile)
    yc = _mixer_c(x, p["w_c"], p["conv_c"], p["conv_bias_c"], p["dt_bias_c"], p["a_log_c"], p["d_skip_c"],
                  p["norm_c"], consts["expand"], seq_tile)
    x1, x1b, ids, wts, cnt = _merge(x.reshape(t, d), ya.reshape(t, -1), yb.reshape(t, -1), yc.reshape(t, -1),
                                    p["w_gate_in"], p["b_gate"], p["w_branch_a"], p["w_branch_b"],
                                    p["w_branch_c"], p["w_out"], p["ln1_g"], p["ln1_b"], p["w_router_hi"],
                                    p["w_router_lo"], p["b_router"], token_tile)
    counts = cnt[0, N_GROUPS:N_GROUPS + N_EXPERTS].astype(jnp.int32)
    row_tok, pos, tile_e, n_valid = _dispatch(ids, counts, t, expert_tile)
    xs = _rows(x1b, row_tok)
    ys = _experts(tile_e, n_valid, xs, p["w_gate_e"], p["w_up_e"], p["w_down_e"], expert_tile)
    g = _rows(ys, pos.reshape(-1)).reshape(t, 2 * d)
    x2 = _combine(x1, g, wts, p["ln2_g"], p["ln2_b"], token_tile)
    return x2.reshape(bsz, s, d)


def _layer_params(l, w_in, conv_a, a_log_a, dt_bias_a, norm_a, norm_b, conv_c, conv_bias_c, dt_bias_c,
                  a_log_c, d_skip_c, norm_c, b_gate, w_branch_a, w_branch_b, w_branch_c, w_out,
                  ln1_g, ln1_b, w_router_group, b_router_group, w_router_expert, b_router_expert,
                  w_gate_e, w_up_e, w_down_e, ln2_g, ln2_b):
    o = IN_OFFS
    w = w_in[l]
    d = w.shape[0]
    zpad = lambda n: jnp.zeros((d, n), F32)
    w_a = jnp.concatenate([w[:, o[0]:o[2]], w[:, o[2]:o[3]], zpad(LANES - A_HEADS),
                           w[:, o[3]:o[4]], zpad(LANES - A_HEADS)], axis=1).astype(BF16)
    w_b = w[:, o[4]:o[8]].astype(BF16)
    w_c = jnp.concatenate([w[:, o[8]:o[10]], w[:, o[10]:o[11]], zpad(LANES - C_HEADS)], axis=1).astype(BF16)
    w_router = jnp.concatenate([w_router_group[l], w_router_expert[l],
                                zpad(LANES - N_GROUPS - N_EXPERTS)], axis=1)
    b_router = _pad_lanes(jnp.concatenate([b_router_group[l], b_router_expert[l]]))
    return {
        "w_a": w_a, "w_b": w_b, "w_c": w_c, "w_gate_in": w[:, o[11]:o[12]].astype(BF16),
        "conv_a": conv_a[l], "a_log_a": _pad_lanes(a_log_a[l]), "dt_bias_a": _pad_lanes(dt_bias_a[l]),
        "norm_a": norm_a[l].reshape(1, -1), "norm_b": norm_b[l].reshape(1, -1),
        "conv_c": conv_c[l], "conv_bias_c": conv_bias_c[l].reshape(1, -1),
        "dt_bias_c": _pad_lanes(dt_bias_c[l]), "a_log_c": _pad_lanes(a_log_c[l]),
        "d_skip_c": jnp.repeat(d_skip_c[l], C_HEAD_DIM).reshape(1, -1), "norm_c": norm_c[l].reshape(1, -1),
        "b_gate": b_gate[l],
        "w_branch_a": w_branch_a[l].astype(BF16), "w_branch_b": w_branch_b[l].astype(BF16),
        "w_branch_c": w_branch_c[l].astype(BF16), "w_out": w_out[l].astype(BF16),
        "ln1_g": ln1_g[l].reshape(1, -1), "ln1_b": ln1_b[l].reshape(1, -1),
        "w_router_hi": w_router.astype(BF16),
        "w_router_lo": (w_router - w_router.astype(BF16).astype(F32)).astype(BF16), "b_router": b_router,
        "w_gate_e": w_gate_e[l].astype(BF16), "w_up_e": w_up_e[l].astype(BF16),
        "w_down_e": w_down_e[l].astype(BF16),
        "ln2_g": ln2_g[l].reshape(1, -1), "ln2_b": ln2_b[l].reshape(1, -1),
    }


def _forward(x, params, seq_tile=SEQ_TILE, token_tile=TOKEN_TILE, expert_tile=EXPERT_TILE):
    s = x.shape[1]
    cos2, sin2 = _rope_tables(s)
    head_of_lane = jnp.arange(C_INNER) // C_HEAD_DIM
    expand = (jnp.arange(LANES)[:, None] == head_of_lane[None, :]).astype(BF16)
    consts = {"cos2": cos2, "sin2": sin2, "ret": _retention_tables(), "expand": expand}
    for l in range(DEPTH):
        x = _layer(x, _layer_params(l, *params), consts, seq_tile, token_tile, expert_tile)
    return x


def kernel(x, w_in, conv_a, a_log_a, dt_bias_a, norm_a, norm_b, conv_c, conv_bias_c, dt_bias_c, a_log_c, d_skip_c, norm_c, b_gate, w_branch_a, w_branch_b, w_branch_c, w_out, ln1_g, ln1_b, w_router_group, b_router_group, w_router_expert, b_router_expert, w_gate_e, w_up_e, w_down_e, ln2_g, ln2_b):
    params = (w_in, conv_a, a_log_a, dt_bias_a, norm_a, norm_b, conv_c, conv_bias_c, dt_bias_c, a_log_c,
              d_skip_c, norm_c, b_gate, w_branch_a, w_branch_b, w_branch_c, w_out, ln1_g, ln1_b,
              w_router_group, b_router_group, w_router_expert, b_router_expert, w_gate_e, w_up_e, w_down_e,
              ln2_g, ln2_b)
    return _forward(x, params)
```

```python
import functools
import math

import jax
import jax.numpy as jnp
from jax import lax
from jax.experimental import pallas as pl
from jax.experimental.pallas import tpu as pltpu

F32 = jnp.float32
BF16 = jnp.bfloat16
HIGHEST = lax.Precision.HIGHEST

D_MODEL = 1024
DEPTH = 2
CHUNK = 64
CONV_K = 4
EPS = 1e-6
LANES = 128
A_HEADS = 4
A_DK = 128
A_DV = 128
A_QKV_W = 3 * A_HEADS * A_DK
A_V_W = A_HEADS * A_DV
B_HEADS = 4
B_DK = 128
B_DV = 128
B_W = B_HEADS * B_DK
ROPE_BASE = 10000.0
C_INNER = 1024
C_HEAD_DIM = 64
C_HEADS = 16
C_GROUPS = 2
C_STATE = 128
C_XBC_W = C_INNER + 2 * C_GROUPS * C_STATE
C_GROUP_W = C_INNER // C_GROUPS
N_BRANCH = 3
N_GROUPS = 4
EXPERTS_PER_GROUP = 8
N_EXPERTS = 32
D_EXPERT = 512
DEEPNORM_ALPHA = (2 * DEPTH) ** 0.25

IN_SIZES = (A_QKV_W, A_V_W, A_HEADS, A_HEADS, B_W, B_W, B_W, B_W, C_INNER, C_XBC_W, C_HEADS,
            N_BRANCH * D_MODEL)
IN_OFFS = tuple(int(sum(IN_SIZES[:i])) for i in range(len(IN_SIZES) + 1))

SEQ_TILE = 512
TOKEN_TILE = 512
EXPERT_TILE = 256
VMEM_LIMIT = 56 * 1024 * 1024


def _dot(a, b):
    return jnp.dot(a, b, preferred_element_type=F32)


def _split3(a):
    hi = a.astype(BF16)
    r1 = a - hi.astype(F32)
    mid = r1.astype(BF16)
    lo = (r1 - mid.astype(F32)).astype(BF16)
    return hi, mid, lo


def _split2(a):
    hi = a.astype(BF16)
    return hi, (a - hi.astype(F32)).astype(BF16)


def _bmm(a, b):
    return jnp.einsum('cij,cjk->cik', a, b, preferred_element_type=F32)


def _dot_sel_rhs(a, sel):
    hi, lo = _split2(a)
    return _dot(hi, sel) + _dot(lo, sel)


def _dot_sel_lhs(sel, a):
    hi, mid, lo = _split3(a)
    return _dot(sel, hi) + _dot(sel, mid) + _dot(sel, lo)


def _silu(x):
    return x * jax.nn.sigmoid(x)


def _softplus(x):
    return jnp.maximum(x, 0.0) + jnp.log(1.0 + jnp.exp(-jnp.abs(x)))


def _chunk_tri(ts, dtype):
    r = lax.broadcasted_iota(jnp.int32, (ts, ts), 0)
    c = lax.broadcasted_iota(jnp.int32, (ts, ts), 1)
    return jnp.where((r // CHUNK == c // CHUNK) & (c <= r), 1.0, 0.0).astype(dtype)


def _causal_conv(buf_ref, pre, conv_ref, ts):
    buf_ref[8:8 + ts, :] = pre
    acc = buf_ref[pl.ds(8 - (CONV_K - 1), ts), :] * conv_ref[0:1, :]
    for k in range(1, CONV_K):
        acc = acc + buf_ref[pl.ds(8 - (CONV_K - 1) + k, ts), :] * conv_ref[k:k + 1, :]
    buf_ref[8 - (CONV_K - 1):8, :] = buf_ref[8 + ts - (CONV_K - 1):8 + ts, :]
    return acc


def _mixer_a_kernel(x_ref, w_ref, conv_ref, alog_ref, dtb_ref, norm_ref, y_ref, buf_ref, state_ref, *, ts):
    nct = ts // CHUNK

    @pl.when(pl.program_id(1) == 0)
    def _():
        buf_ref[0:8, :] = jnp.zeros((8, A_QKV_W), F32)
        state_ref[...] = jnp.zeros_like(state_ref)

    xb = x_ref[0].astype(BF16)
    proj = _dot(xb, w_ref[...])
    qkv = _silu(_causal_conv(buf_ref, proj[:, :A_QKV_W], conv_ref, ts))
    z = proj[:, A_QKV_W:A_QKV_W + A_V_W]
    a_raw = proj[:, 2048:2048 + LANES]
    b_raw = proj[:, 2048 + LANES:2048 + 2 * LANES]
    beta = jax.nn.sigmoid(b_raw)
    g = -jnp.exp(alog_ref[...]) * _softplus(a_raw + dtb_ref[...])
    gc = _dot_sel_lhs(_chunk_tri(ts, BF16), g)
    gc3 = gc.reshape(nct, CHUNK, LANES)
    beta3 = beta.reshape(nct, CHUNK, LANES)
    gct3 = jnp.swapaxes(gc3, 1, 2)
    ri = lax.broadcasted_iota(jnp.int32, (CHUNK, CHUNK), 0)
    ci = lax.broadcasted_iota(jnp.int32, (CHUNK, CHUNK), 1)
    strict = (ci < ri)[None]
    eye = jnp.where(ri == ci, 1.0, 0.0).astype(F32)[None]
    nh = A_HEADS

    def stack(per_head):
        parts = [per_head(h) for h in range(nh)]
        return jnp.concatenate([parts[h][c:c + 1] for c in range(nct) for h in range(nh)], axis=0)

    q = stack(lambda h: qkv[:, h * A_DK:(h + 1) * A_DK].reshape(nct, CHUNK, A_DK))
    k = stack(lambda h: qkv[:, (nh + h) * A_DK:(nh + h + 1) * A_DK].reshape(nct, CHUNK, A_DK))
    v = stack(lambda h: qkv[:, 2 * nh * A_DK + h * A_DV:2 * nh * A_DK + (h + 1) * A_DV].reshape(nct, CHUNK, A_DV))
    zs = stack(lambda h: z[:, h * A_DV:(h + 1) * A_DV].reshape(nct, CHUNK, A_DV))
    gcol = stack(lambda h: gc3[:, :, h:h + 1])
    grow = stack(lambda h: gct3[:, h:h + 1, :])
    bcol = stack(lambda h: beta3[:, :, h:h + 1])
    glast = stack(lambda h: gc3[:, CHUNK - 1:CHUNK, h:h + 1])
    q = q * lax.rsqrt(jnp.sum(q * q, axis=-1, keepdims=True) + EPS) * (A_DK ** -0.5)
    k = k * lax.rsqrt(jnp.sum(k * k, axis=-1, keepdims=True) + EPS)
    decay = jnp.where(strict, jnp.exp(jnp.where(strict, gcol - grow, 0.0)), 0.0)
    kb = k.astype(BF16)
    amat = bcol * jnp.einsum('bid,bjd->bij', kb, kb, preferred_element_type=F32) * decay
    xinv = eye - amat
    pw = amat.astype(BF16)
    for _ in range(5):
        pw = _bmm(pw, pw).astype(BF16)
        xinv = xinv + _bmm(xinv.astype(BF16), pw)
    ah, al = _split2(amat)
    xh, xl = _split2(xinv)
    resid = eye - xinv - (_bmm(ah, xh) + _bmm(ah, xl) + _bmm(al, xh))
    xinv = xinv + _bmm(xh, resid.astype(BF16))
    rhs = jnp.concatenate([v * bcol, k * (bcol * jnp.exp(gcol))], axis=-1)
    sol = _bmm(xinv.astype(BF16), rhs.astype(BF16))
    u = sol[:, :, :A_DV]
    wq = jnp.concatenate([sol[:, :, A_DV:], q], axis=1).astype(BF16)
    kend = (k * jnp.exp(glast - gcol)).astype(BF16)
    qk = jnp.einsum('bid,bjd->bij', q.astype(BF16), kend, preferred_element_type=F32).astype(BF16)
    kend_t = jnp.swapaxes(kend, 1, 2)
    dec = jnp.exp(glast)
    s = state_ref[...]
    outs = []
    for c in range(nct):
        sl = slice(c * nh, (c + 1) * nh)
        r = _bmm(wq[sl], s.astype(BF16))
        delta = (u[sl] - r[:, :CHUNK]).astype(BF16)
        outs.append(dec[sl] * r[:, CHUNK:] + _bmm(qk[sl], delta))
        s = dec[sl] * s + _bmm(kend_t[sl], delta)
    state_ref[...] = s
    o = jnp.concatenate(outs, axis=0)
    o = o * lax.rsqrt(jnp.mean(o * o, axis=-1, keepdims=True) + EPS) * norm_ref[...]
    o = (o * _silu(zs)).astype(y_ref.dtype)
    for c in range(nct):
        for h in range(nh):
            y_ref[0, c * CHUNK:(c + 1) * CHUNK, h * A_DV:(h + 1) * A_DV] = o[c * nh + h]


def _mixer_a(x, w, conv, alog, dtb, norm, ts):
    bsz, s, d = x.shape
    wcols = w.shape[1]
    full = lambda shape: pl.BlockSpec(shape, lambda b, j: (0,) * len(shape))
    return pl.pallas_call(
        functools.partial(_mixer_a_kernel, ts=ts),
        grid=(bsz, s // ts),
        in_specs=[pl.BlockSpec((1, ts, d), lambda b, j: (b, j, 0)),
                  full((d, wcols)), full((CONV_K, A_QKV_W)), full((1, LANES)), full((1, LANES)),
                  full((1, A_DV))],
        out_specs=pl.BlockSpec((1, ts, A_V_W), lambda b, j: (b, j, 0)),
        out_shape=jax.ShapeDtypeStruct((bsz, s, A_V_W), BF16),
        scratch_shapes=[pltpu.VMEM((8 + ts, A_QKV_W), F32), pltpu.VMEM((A_HEADS, A_DK, A_DV), F32)],
        compiler_params=pltpu.CompilerParams(dimension_semantics=("parallel", "arbitrary"),
                                             vmem_limit_bytes=VMEM_LIMIT),
        name="mixer_a",
    )(x, w, conv, alog, dtb, norm)


def _mixer_b_kernel(x_ref, w_ref, cos_ref, sin_ref, intra_ref, read_ref, write_ref, cdec_ref, norm_ref,
                    y_ref, state_ref, *, ts):
    nct = ts // CHUNK

    @pl.when(pl.program_id(1) == 0)
    def _():
        state_ref[...] = jnp.zeros_like(state_ref)

    xb = x_ref[0].astype(BF16)
    proj = _dot(xb, w_ref[...])
    cos = cos_ref[...]
    sin = sin_ref[...]
    nh = B_HEADS

    def stack(per_head):
        parts = [per_head(h) for h in range(nh)]
        return jnp.concatenate([parts[h][c:c + 1] for c in range(nct) for h in range(nh)], axis=0)

    def rope(t):
        return t * cos + pltpu.roll(t, B_DK // 2, 1) * sin

    q = stack(lambda h: rope(proj[:, h * B_DK:(h + 1) * B_DK]).reshape(nct, CHUNK, B_DK)).astype(BF16)
    k = stack(lambda h: (rope(proj[:, B_W + h * B_DK:B_W + (h + 1) * B_DK]) * (B_DK ** -0.5))
              .reshape(nct, CHUNK, B_DK))
    v = stack(lambda h: proj[:, 2 * B_W + h * B_DV:2 * B_W + (h + 1) * B_DV].reshape(nct, CHUNK, B_DV)).astype(BF16)
    gate = stack(lambda h: proj[:, 3 * B_W + h * B_DV:3 * B_W + (h + 1) * B_DV].reshape(nct, CHUNK, B_DV))
    tile_heads = lambda ref: jnp.concatenate([ref[...]] * nct, axis=0)
    scores = jnp.einsum('bid,bjd->bij', q, k.astype(BF16), preferred_element_type=F32) * tile_heads(intra_ref)
    o = _bmm(scores.astype(BF16), v)
    kw_t = jnp.swapaxes((k * tile_heads(write_ref)).astype(BF16), 1, 2)
    kv = _bmm(kw_t, v)
    s = state_ref[...]
    cdec = cdec_ref[...]
    prev = []
    for c in range(nct):
        prev.append(s.astype(BF16))
        s = cdec * s + kv[c * nh:(c + 1) * nh]
    state_ref[...] = s
    o = o + _bmm(q, jnp.concatenate(prev, axis=0)) * tile_heads(read_ref)
    mu = jnp.mean(o, axis=-1, keepdims=True)
    oc = o - mu
    var = jnp.mean(oc * oc, axis=-1, keepdims=True)
    norm = jnp.concatenate([norm_ref[:, h * B_DV:(h + 1) * B_DV][None] for h in range(nh)] * nct, axis=0)
    o = (oc * lax.rsqrt(var + EPS) * norm * _silu(gate)).astype(y_ref.dtype)
    for c in range(nct):
        for h in range(nh):
            y_ref[0, c * CHUNK:(c + 1) * CHUNK, h * B_DV:(h + 1) * B_DV] = o[c * nh + h]


def _retention_tables():
    log_gamma = jnp.log1p(-jnp.exp2(-5.0 - jnp.arange(B_HEADS, dtype=F32)))
    idx = jnp.arange(CHUNK, dtype=F32)
    intra = jnp.exp(log_gamma[:, None, None] * jnp.abs(idx[:, None] - idx[None, :]))
    read = jnp.broadcast_to(jnp.exp(log_gamma[:, None] * (idx + 1.0))[:, :, None], (B_HEADS, CHUNK, B_DV))
    write = jnp.broadcast_to(jnp.exp(log_gamma[:, None] * (CHUNK - 1.0 - idx))[:, :, None], (B_HEADS, CHUNK, B_DK))
    cdec = jnp.broadcast_to(jnp.exp(log_gamma * CHUNK)[:, None, None], (B_HEADS, 1, B_DV))
    return intra, read, write, cdec


def _rope_tables(s):
    pos = jnp.arange(s, dtype=F32)
    inv_freq = ROPE_BASE ** (-jnp.arange(0, B_DK, 2, dtype=F32) / B_DK)
    ang = pos[:, None] * inv_freq[None, :]
    cos, sin = jnp.cos(ang), jnp.sin(ang)
    return jnp.concatenate([cos, cos], axis=-1), jnp.concatenate([-sin, sin], axis=-1)


def _mixer_b(x, w, cos2, sin2, tables, norm, ts):
    bsz, s, d = x.shape
    intra, read, write, cdec = tables
    full = lambda shape: pl.BlockSpec(shape, lambda b, j: (0,) * len(shape))
    return pl.pallas_call(
        functools.partial(_mixer_b_kernel, ts=ts),
        grid=(bsz, s // ts),
        in_specs=[pl.BlockSpec((1, ts, d), lambda b, j: (b, j, 0)),
                  full((d, 4 * B_W)),
                  pl.BlockSpec((ts, B_DK), lambda b, j: (j, 0)), pl.BlockSpec((ts, B_DK), lambda b, j: (j, 0)),
                  full(intra.shape), full(read.shape), full(write.shape), full(cdec.shape), full((1, B_W))],
        out_specs=pl.BlockSpec((1, ts, B_W), lambda b, j: (b, j, 0)),
        out_shape=jax.ShapeDtypeStruct((bsz, s, B_W), BF16),
        scratch_shapes=[pltpu.VMEM((B_HEADS, B_DK, B_DV), F32)],
        compiler_params=pltpu.CompilerParams(dimension_semantics=("parallel", "arbitrary"),
                                             vmem_limit_bytes=VMEM_LIMIT),
        name="mixer_b",
    )(x, w, cos2, sin2, intra, read, write, cdec, norm)


def _mixer_c_kernel(x_ref, w_ref, conv_ref, convb_ref, dtb_ref, alog_ref, dskip_ref, norm_ref, expand_ref,
                    y_ref, buf_ref, state_ref, *, ts):
    nct = ts // CHUNK

    @pl.when(pl.program_id(1) == 0)
    def _():
        buf_ref[0:8, :] = jnp.zeros((8, C_XBC_W), F32)
        state_ref[...] = jnp.zeros_like(state_ref)

    xb = x_ref[0].astype(BF16)
    proj = _dot(xb, w_ref[...])
    z = proj[:, :C_INNER]
    xbc = _silu(_causal_conv(buf_ref, proj[:, C_INNER:C_INNER + C_XBC_W], conv_ref, ts) + convb_ref[...])
    xs = xbc[:, :C_INNER]
    bm = xbc[:, C_INNER:C_INNER + C_GROUPS * C_STATE]
    cm = xbc[:, C_INNER + C_GROUPS * C_STATE:]
    dt = _softplus(proj[:, C_INNER + C_XBC_W:] + dtb_ref[...])
    lstep = dt * (-jnp.exp(alog_ref[...]))
    lc = _dot_sel_lhs(_chunk_tri(ts, BF16), lstep)
    expand = expand_ref[...]
    lce = _dot_sel_rhs(lc, expand)
    dte = _dot_sel_rhs(dt, expand)
    ng = C_GROUPS

    def stack(per_group):
        parts = [per_group(g) for g in range(ng)]
        return jnp.concatenate([parts[g][c:c + 1] for c in range(nct) for g in range(ng)], axis=0)

    def group_lanes(a, width=C_GROUP_W):
        return stack(lambda g: a[:, g * width:(g + 1) * width].reshape(nct, CHUNK, width))

    def group_rows(ref):
        return jnp.concatenate([ref[:, g * C_GROUP_W:(g + 1) * C_GROUP_W][None] for g in range(ng)] * nct, axis=0)

    lg = group_lanes(lce)
    dg = group_lanes(dte)
    xg = group_lanes(xs)
    zg = group_lanes(z)
    bmg = group_lanes(bm, C_STATE).astype(BF16)
    cmg = group_lanes(cm, C_STATE).astype(BF16)
    heads_per_group = C_HEADS // ng

    def row_form(a):
        at = jnp.swapaxes(a.reshape(nct, CHUNK, LANES), 1, 2)
        return stack(lambda g: jnp.concatenate(
            [at[:, g * heads_per_group + h:g * heads_per_group + h + 1, :] for h in range(heads_per_group)], axis=-1))

    lc_row = row_form(lc)
    dt_row = row_form(dt)
    bm_t = jnp.concatenate([bmg] * (C_GROUP_W // CHUNK), axis=1)
    cbt = jnp.einsum('bin,bmn->bim', cmg, bm_t, preferred_element_type=F32)
    wm = (cbt * jnp.exp(-jnp.abs(lg - lc_row)) * dt_row).astype(BF16)
    lane = lax.broadcasted_iota(jnp.int32, (1, 1, LANES), 2)
    left = lane < C_HEAD_DIM
    pair_out = []
    for p in range(C_GROUP_W // LANES):
        xp = xg[:, :, p * LANES:(p + 1) * LANES]
        bd = jnp.concatenate([jnp.where(left, xp, 0.0), jnp.where(left, 0.0, xp)], axis=1)
        pair_out.append(_bmm(wm[:, :, p * LANES:(p + 1) * LANES], bd.astype(BF16)))
    y = jnp.concatenate(pair_out, axis=-1)
    llast = lg[:, CHUNK - 1:CHUNK, :]
    xw = (xg * (jnp.exp(llast - lg) * dg)).astype(BF16)
    kv = _bmm(jnp.swapaxes(bmg, 1, 2), xw)
    sdec = jnp.exp(llast)
    st = state_ref[...]
    prev = []
    for ch in range(nct):
        sl = slice(ch * ng, (ch + 1) * ng)
        prev.append(st.astype(BF16))
        st = sdec[sl] * st + kv[sl]
    state_ref[...] = st
    y = y + _bmm(cmg, jnp.concatenate(prev, axis=0)) * jnp.exp(lg)
    y = (y + group_rows(dskip_ref) * xg) * _silu(zg)
    y = (y * lax.rsqrt(jnp.mean(y * y, axis=-1, keepdims=True) + EPS) * group_rows(norm_ref)).astype(y_ref.dtype)
    for ch in range(nct):
        for g in range(ng):
            y_ref[0, ch * CHUNK:(ch + 1) * CHUNK, g * C_GROUP_W:(g + 1) * C_GROUP_W] = y[ch * ng + g]


def _mixer_c(x, w, conv, convb, dtb, alog, dskip, norm, expand, ts):
    bsz, s, d = x.shape
    wcols = w.shape[1]
    full = lambda shape: pl.BlockSpec(shape, lambda b, j: (0,) * len(shape))
    return pl.pallas_call(
        functools.partial(_mixer_c_kernel, ts=ts),
        grid=(bsz, s // ts),
        in_specs=[pl.BlockSpec((1, ts, d), lambda b, j: (b, j, 0)),
                  full((d, wcols)), full((CONV_K, C_XBC_W)), full((1, C_XBC_W)), full((1, LANES)),
                  full((1, LANES)), full((1, C_INNER)), full((1, C_INNER)), full((LANES, C_INNER))],
        out_specs=pl.BlockSpec((1, ts, C_INNER), lambda b, j: (b, j, 0)),
        out_shape=jax.ShapeDtypeStruct((bsz, s, C_INNER), BF16),
        scratch_shapes=[pltpu.VMEM((8 + ts, C_XBC_W), F32), pltpu.VMEM((C_GROUPS, C_STATE, C_GROUP_W), F32)],
        compiler_params=pltpu.CompilerParams(dimension_semantics=("parallel", "arbitrary"),
                                             vmem_limit_bytes=VMEM_LIMIT),
        name="mixer_c",
    )(x, w, conv, convb, dtb, alog, dskip, norm, expand)


def _layernorm(h, g, b):
    mu = jnp.mean(h, axis=-1, keepdims=True)
    hc = h - mu
    var = jnp.mean(hc * hc, axis=-1, keepdims=True)
    return hc * lax.rsqrt(var + EPS) * g + b


def _merge_kernel(x_ref, ya_ref, yb_ref, yc_ref, wg_ref, bg_ref, wa_ref, wb_ref, wc_ref, wo_ref,
                  lng_ref, lnb_ref, wrh_ref, wrl_ref, br_ref, x1_ref, x1b_ref, ids_ref, wts_ref, cnt_ref,
                  run_ref):
    @pl.when(pl.program_id(0) == 0)
    def _():
        run_ref[...] = jnp.zeros_like(run_ref)

    x = x_ref[...]
    xb = x.astype(BF16)
    merged = None
    for i, (y_ref, w_ref) in enumerate(((ya_ref, wa_ref), (yb_ref, wb_ref), (yc_ref, wc_ref))):
        gate = jax.nn.sigmoid(_dot(xb, wg_ref[:, i * D_MODEL:(i + 1) * D_MODEL]) + bg_ref[i:i + 1, :])
        term = gate * _dot(y_ref[...], w_ref[...])
        merged = term if merged is None else merged + term
    mix = _dot(merged.astype(BF16), wo_ref[...])
    x1 = _layernorm(DEEPNORM_ALPHA * x + mix, lng_ref[...], lnb_ref[...])
    x1_ref[...] = x1
    x1b_ref[...] = x1.astype(BF16)

    xh, xl = _split2(x1)
    logits = _dot(xh, wrh_ref[...]) + _dot(xl, wrh_ref[...]) + _dot(xh, wrl_ref[...]) + br_ref[...]
    tm = logits.shape[0]
    lane = lax.broadcasted_iota(jnp.int32, logits.shape, 1)
    neg = jnp.float32(-jnp.inf)
    gmask = lane < N_GROUPS
    gl = jnp.where(gmask, logits, neg)
    gmax = jnp.max(gl, axis=-1, keepdims=True)
    gidx = jnp.min(jnp.where(gmask & (gl == gmax), lane, LANES), axis=-1, keepdims=True)
    gp = 1.0 / jnp.sum(jnp.where(gmask, jnp.exp(gl - gmax), 0.0), axis=-1, keepdims=True)
    emask = (lane >= N_GROUPS) & (lane < N_GROUPS + N_EXPERTS) & ((lane - N_GROUPS) // EXPERTS_PER_GROUP == gidx)
    el = jnp.where(emask, logits, neg)
    v1 = jnp.max(el, axis=-1, keepdims=True)
    i1 = jnp.min(jnp.where(emask & (el == v1), lane, LANES), axis=-1, keepdims=True)
    el2 = jnp.where(lane == i1, neg, el)
    v2 = jnp.max(el2, axis=-1, keepdims=True)
    i2 = jnp.min(jnp.where(emask & (lane != i1) & (el2 == v2), lane, LANES), axis=-1, keepdims=True)
    e21 = jnp.exp(v2 - v1)
    w1 = gp / (1.0 + e21)
    w2 = gp * e21 / (1.0 + e21)

    hot1 = lane == i1
    hot2 = lane == i2
    hot = jnp.where(hot1 | hot2, 1.0, 0.0)
    r = lax.broadcasted_iota(jnp.int32, (tm, tm), 0)
    c = lax.broadcasted_iota(jnp.int32, (tm, tm), 1)
    before = _dot(jnp.where(c < r, 1.0, 0.0).astype(BF16), hot.astype(BF16)) + run_ref[0:1, :]
    rank1 = jnp.sum(jnp.where(hot1, before, 0.0), axis=-1, keepdims=True)
    rank2 = jnp.sum(jnp.where(hot2, before, 0.0), axis=-1, keepdims=True)
    run_ref[0:1, :] = run_ref[0:1, :] + jnp.sum(hot, axis=0, keepdims=True)
    cnt_ref[...] = jnp.broadcast_to(run_ref[0:1, :], cnt_ref.shape)

    ids = jnp.where(lane == 0, (i1 - N_GROUPS).astype(F32), jnp.where(lane == 1, (i2 - N_GROUPS).astype(F32),
                    jnp.where(lane == 2, rank1, jnp.where(lane == 3, rank2, 0.0))))
    ids_ref[...] = jnp.transpose(ids)[0:8, :].astype(jnp.int32)
    wts_ref[...] = jnp.where(lane == 0, w1, jnp.where(lane == 1, w2, 0.0))


def _merge(x2d, ya, yb, yc, wg, bg, wa, wb, wc, wo, lng, lnb, wrh, wrl, br, tm):
    t, d = x2d.shape
    full = lambda shape: pl.BlockSpec(shape, lambda i: (0,) * len(shape))
    row = lambda w: pl.BlockSpec((tm, w), lambda i: (i, 0))
    return pl.pallas_call(
        _merge_kernel,
        grid=(t // tm,),
        in_specs=[row(d), row(A_V_W), row(B_W), row(C_INNER),
                  full(wg.shape), full(bg.shape), full(wa.shape), full(wb.shape), full(wc.shape), full(wo.shape),
                  full((1, d)), full((1, d)), full(wrh.shape), full(wrl.shape), full((1, LANES))],
        out_specs=[row(d), row(d), pl.BlockSpec((8, tm), lambda i: (0, i)), row(LANES), full((8, LANES))],
        out_shape=[jax.ShapeDtypeStruct((t, d), F32), jax.ShapeDtypeStruct((t, d), BF16),
                   jax.ShapeDtypeStruct((8, t), jnp.int32), jax.ShapeDtypeStruct((t, LANES), F32),
                   jax.ShapeDtypeStruct((8, LANES), F32)],
        scratch_shapes=[pltpu.VMEM((8, LANES), F32)],
        compiler_params=pltpu.CompilerParams(dimension_semantics=("arbitrary",), vmem_limit_bytes=VMEM_LIMIT),
        name="merge_ln_router",
    )(x2d, ya, yb, yc, wg, bg, wa, wb, wc, wo, lng, lnb, wrh, wrl, br)


def _experts_kernel(te_ref, nv_ref, xs_ref, wg_ref, wu_ref, wd_ref, ys_ref, wgb_ref, wub_ref, wdb_ref):
    i = pl.program_id(0)

    @pl.when((i == 0) | (te_ref[i] != te_ref[jnp.maximum(i - 1, 0)]))
    def _():
        wgb_ref[...] = wg_ref[0, 0].astype(BF16)
        wub_ref[...] = wu_ref[0, 0].astype(BF16)
        wdb_ref[...] = wd_ref[0, 0].astype(BF16)

    @pl.when(i < nv_ref[0])
    def _():
        xs = xs_ref[...]
        h = _silu(_dot(xs, wgb_ref[...])) * _dot(xs, wub_ref[...])
        ys_ref[...] = _dot(h.astype(BF16), wdb_ref[...]).astype(ys_ref.dtype)

    @pl.when(i >= nv_ref[0])
    def _():
        ys_ref[...] = jnp.zeros_like(ys_ref)


def _experts(tile_expert, n_valid, xs, wg, wu, wd, layer, tm):
    rows, d = xs.shape
    grid_spec = pltpu.PrefetchScalarGridSpec(
        num_scalar_prefetch=2,
        grid=(rows // tm,),
        in_specs=[pl.BlockSpec((tm, d), lambda i, te, nv: (i, 0)),
                  pl.BlockSpec((1, 1, d, D_EXPERT), lambda i, te, nv: (layer, te[i], 0, 0)),
                  pl.BlockSpec((1, 1, d, D_EXPERT), lambda i, te, nv: (layer, te[i], 0, 0)),
                  pl.BlockSpec((1, 1, D_EXPERT, d), lambda i, te, nv: (layer, te[i], 0, 0))],
        out_specs=pl.BlockSpec((tm, d), lambda i, te, nv: (i, 0)),
        scratch_shapes=[pltpu.VMEM((d, D_EXPERT), BF16), pltpu.VMEM((d, D_EXPERT), BF16),
                        pltpu.VMEM((D_EXPERT, d), BF16)],
    )
    return pl.pallas_call(
        _experts_kernel,
        grid_spec=grid_spec,
        out_shape=jax.ShapeDtypeStruct((rows, d), BF16),
        compiler_params=pltpu.CompilerParams(dimension_semantics=("arbitrary",), vmem_limit_bytes=VMEM_LIMIT),
        name="grouped_experts",
    )(tile_expert, n_valid, xs, wg, wu, wd)


def _combine_kernel(x_ref, g_ref, wts_ref, lng_ref, lnb_ref, o_ref):
    wts = wts_ref[...]
    ffn = wts[:, 0:1] * g_ref[0].astype(F32) + wts[:, 1:2] * g_ref[1].astype(F32)
    o_ref[...] = _layernorm(DEEPNORM_ALPHA * x_ref[...] + ffn, lng_ref[...], lnb_ref[...])


def _combine(x1, g, wts, lng, lnb, tm):
    t, d = x1.shape
    row = lambda w: pl.BlockSpec((tm, w), lambda i: (i, 0))
    vec = pl.BlockSpec((1, d), lambda i: (0, 0))
    return pl.pallas_call(
        _combine_kernel,
        grid=(t // tm,),
        in_specs=[row(d), pl.BlockSpec((2, tm, d), lambda i: (0, i, 0)), row(LANES), vec, vec],
        out_specs=row(d),
        out_shape=jax.ShapeDtypeStruct((t, d), F32),
        compiler_params=pltpu.CompilerParams(dimension_semantics=("parallel",), vmem_limit_bytes=VMEM_LIMIT),
        name="combine_ln",
    )(x1, g, wts, lng, lnb)


def _pad_lanes(v, width=LANES):
    v = v.reshape(1, -1).astype(F32)
    return jnp.pad(v, ((0, 0), (0, width - v.shape[1])))


def _rows(a, idx):
    return a.at[idx].get(mode="promise_in_bounds")


def _dispatch(ids, counts, t, tm):
    n_pairs = 2 * t
    n_tiles = n_pairs // tm + N_EXPERTS
    e1, e2, rank1, rank2 = ids[0], ids[1], ids[2], ids[3]
    padded = ((counts + tm - 1) // tm) * tm
    pend = jnp.cumsum(padded)
    pstart = pend - padded
    start = jnp.cumsum(counts) - counts
    pos = jnp.concatenate([_rows(pstart, e1) + rank1, _rows(pstart, e2) + rank2])
    tok2 = 2 * jnp.arange(t, dtype=jnp.int32)
    key = jnp.concatenate([e1 * n_pairs + tok2, e2 * n_pairs + tok2 + 1])
    order = lax.sort(key) % n_pairs
    n_valid = (pend[-1] // tm).astype(jnp.int32).reshape(1)
    tile_start = jnp.arange(n_tiles, dtype=jnp.int32) * tm
    count_le = lambda v: jnp.sum((pend[None, :] <= v[:, None]).astype(jnp.int32), axis=1)
    tile_e = jnp.minimum(count_le(tile_start), count_le(pend[-1:] - 1))
    row = jnp.arange(n_tiles * tm, dtype=jnp.int32)
    row_e = jnp.repeat(tile_e, tm)
    off = row - _rows(pstart, row_e)
    valid = (off >= 0) & (off < _rows(counts, row_e))
    idx = jnp.where(valid, _rows(start, row_e) + off, 0)
    row_tok = jnp.where(valid, _rows(order, idx) // 2, 0)
    return row_tok, pos, tile_e, n_valid


def _layer(x, p, consts, seq_tile, token_tile, expert_tile):
    bsz, s, d = x.shape
    t = bsz * s
    ya = _mixer_a(x, p["w_a"], p["conv_a"], p["a_log_a"], p["dt_bias_a"], p["norm_a"], seq_tile)
    yb = _mixer_b(x, p["w_b"], consts["cos2"], consts["sin2"], consts["ret"], p["norm_b"], seq_tile)
    yc = _mixer_c(x, p["w_c"], p["conv_c"], p["conv_bias_c"], p["dt_bias_c"], p["a_log_c"], p["d_skip_c"],
                  p["norm_c"], consts["expand"], seq_tile)
    x1, x1b, ids, wts, cnt = _merge(x.reshape(t, d), ya.reshape(t, -1), yb.reshape(t, -1), yc.reshape(t, -1),
                                    p["w_gate_in"], p["b_gate"], p["w_branch_a"], p["w_branch_b"],
                                    p["w_branch_c"], p["w_out"], p["ln1_g"], p["ln1_b"], p["w_router_hi"],
                                    p["w_router_lo"], p["b_router"], token_tile)
    counts = cnt[0, N_GROUPS:N_GROUPS + N_EXPERTS].astype(jnp.int32)
    row_tok, pos, tile_e, n_valid = _dispatch(ids, counts, t, expert_tile)
    xs = _rows(x1b, row_tok)
    ys = _experts(tile_e, n_valid, xs, p["w_gate_e"], p["w_up_e"], p["w_down_e"], p["layer"], expert_tile)
    g = _rows(ys, pos).reshape(2, t, d)
    x2 = _combine(x1, g, wts, p["ln2_g"], p["ln2_b"], token_tile)
    return x2.reshape(bsz, s, d)


def _layer_params(l, w_in, conv_a, a_log_a, dt_bias_a, norm_a, norm_b, conv_c, conv_bias_c, dt_bias_c,
                  a_log_c, d_skip_c, norm_c, b_gate, w_branch_a, w_branch_b, w_branch_c, w_out,
                  ln1_g, ln1_b, w_router_group, b_router_group, w_router_expert, b_router_expert,
                  w_gate_e, w_up_e, w_down_e, ln2_g, ln2_b):
    o = IN_OFFS
    w = w_in[l]
    d = w.shape[0]
    zpad = lambda n: jnp.zeros((d, n), F32)
    w_a = jnp.concatenate([w[:, o[0]:o[2]], w[:, o[2]:o[3]], zpad(LANES - A_HEADS),
                           w[:, o[3]:o[4]], zpad(LANES - A_HEADS)], axis=1).astype(BF16)
    w_b = w[:, o[4]:o[8]].astype(BF16)
    w_c = jnp.concatenate([w[:, o[8]:o[10]], w[:, o[10]:o[11]], zpad(LANES - C_HEADS)], axis=1).astype(BF16)
    w_router = jnp.concatenate([w_router_group[l], w_router_expert[l],
                                zpad(LANES - N_GROUPS - N_EXPERTS)], axis=1)
    b_router = _pad_lanes(jnp.concatenate([b_router_group[l], b_router_expert[l]]))
    return {
        "w_a": w_a, "w_b": w_b, "w_c": w_c, "w_gate_in": w[:, o[11]:o[12]].astype(BF16),
        "conv_a": conv_a[l], "a_log_a": _pad_lanes(a_log_a[l]), "dt_bias_a": _pad_lanes(dt_bias_a[l]),
        "norm_a": norm_a[l].reshape(1, -1), "norm_b": norm_b[l].reshape(1, -1),
        "conv_c": conv_c[l], "conv_bias_c": conv_bias_c[l].reshape(1, -1),
        "dt_bias_c": _pad_lanes(dt_bias_c[l]), "a_log_c": _pad_lanes(a_log_c[l]),
        "d_skip_c": jnp.repeat(d_skip_c[l], C_HEAD_DIM).reshape(1, -1), "norm_c": norm_c[l].reshape(1, -1),
        "b_gate": b_gate[l],
        "w_branch_a": w_branch_a[l].astype(BF16), "w_branch_b": w_branch_b[l].astype(BF16),
        "w_branch_c": w_branch_c[l].astype(BF16), "w_out": w_out[l].astype(BF16),
        "ln1_g": ln1_g[l].reshape(1, -1), "ln1_b": ln1_b[l].reshape(1, -1),
        "w_router_hi": w_router.astype(BF16),
        "w_router_lo": (w_router - w_router.astype(BF16).astype(F32)).astype(BF16), "b_router": b_router,
        "w_gate_e": w_gate_e, "w_up_e": w_up_e, "w_down_e": w_down_e, "layer": l,
        "ln2_g": ln2_g[l].reshape(1, -1), "ln2_b": ln2_b[l].reshape(1, -1),
    }


def _forward(x, params, seq_tile=SEQ_TILE, token_tile=TOKEN_TILE, expert_tile=EXPERT_TILE):
    s = x.shape[1]
    cos2, sin2 = _rope_tables(s)
    head_of_lane = jnp.arange(C_INNER) // C_HEAD_DIM
    expand = (jnp.arange(LANES)[:, None] == head_of_lane[None, :]).astype(BF16)
    consts = {"cos2": cos2, "sin2": sin2, "ret": _retention_tables(), "expand": expand}
    for l in range(DEPTH):
        x = _layer(x, _layer_params(l, *params), consts, seq_tile, token_tile, expert_tile)
    return x


def kernel(x, w_in, conv_a, a_log_a, dt_bias_a, norm_a, norm_b, conv_c, conv_bias_c, dt_bias_c, a_log_c, d_skip_c, norm_c, b_gate, w_branch_a, w_branch_b, w_branch_c, w_out, ln1_g, ln1_b, w_router_group, b_router_group, w_router_expert, b_router_expert, w_gate_e, w_up_e, w_down_e, ln2_g, ln2_b):
    params = (w_in, conv_a, a_log_a, dt_bias_a, norm_a, norm_b, conv_c, conv_bias_c, dt_bias_c, a_log_c,
              d_skip_c, norm_c, b_gate, w_branch_a, w_branch_b, w_branch_c, w_out, ln1_g, ln1_b,
              w_router_group, b_router_group, w_router_expert, b_router_expert, w_gate_e, w_up_e, w_down_e,
              ln2_g, ln2_b)
    return _forward(x, params)
```

```python
import functools
import math

import jax
import jax.numpy as jnp
from jax import lax
from jax.experimental import pallas as pl
from jax.experimental.pallas import tpu as pltpu

F32 = jnp.float32
BF16 = jnp.bfloat16
HIGHEST = lax.Precision.HIGHEST

D_MODEL = 1024
DEPTH = 2
CHUNK = 64
CONV_K = 4
EPS = 1e-6
LANES = 128
A_HEADS = 4
A_DK = 128
A_DV = 128
A_QKV_W = 3 * A_HEADS * A_DK
A_V_W = A_HEADS * A_DV
B_HEADS = 4
B_DK = 128
B_DV = 128
B_W = B_HEADS * B_DK
ROPE_BASE = 10000.0
C_INNER = 1024
C_HEAD_DIM = 64
C_HEADS = 16
C_GROUPS = 2
C_STATE = 128
C_XBC_W = C_INNER + 2 * C_GROUPS * C_STATE
C_GROUP_W = C_INNER // C_GROUPS
N_BRANCH = 3
N_GROUPS = 4
EXPERTS_PER_GROUP = 8
N_EXPERTS = 32
D_EXPERT = 512
DEEPNORM_ALPHA = (2 * DEPTH) ** 0.25

IN_SIZES = (A_QKV_W, A_V_W, A_HEADS, A_HEADS, B_W, B_W, B_W, B_W, C_INNER, C_XBC_W, C_HEADS,
            N_BRANCH * D_MODEL)
IN_OFFS = tuple(int(sum(IN_SIZES[:i])) for i in range(len(IN_SIZES) + 1))

SEQ_TILE = 512
TOKEN_TILE = 512
EXPERT_TILE = 512
DISPATCH_CHUNKS = 4
VMEM_LIMIT = 56 * 1024 * 1024


def _dot(a, b):
    return jnp.dot(a, b, preferred_element_type=F32)


def _split3(a):
    hi = a.astype(BF16)
    r1 = a - hi.astype(F32)
    mid = r1.astype(BF16)
    lo = (r1 - mid.astype(F32)).astype(BF16)
    return hi, mid, lo


def _split2(a):
    hi = a.astype(BF16)
    return hi, (a - hi.astype(F32)).astype(BF16)


def _bmm(a, b):
    return jnp.einsum('cij,cjk->cik', a, b, preferred_element_type=F32)


def _dot_sel_rhs(a, sel):
    hi, lo = _split2(a)
    return _dot(hi, sel) + _dot(lo, sel)


def _dot_sel_lhs(sel, a):
    hi, mid, lo = _split3(a)
    return _dot(sel, hi) + _dot(sel, mid) + _dot(sel, lo)


def _silu(x):
    return x * jax.nn.sigmoid(x)


def _softplus(x):
    return jnp.maximum(x, 0.0) + jnp.log(1.0 + jnp.exp(-jnp.abs(x)))


def _chunk_tri(ts, dtype):
    r = lax.broadcasted_iota(jnp.int32, (ts, ts), 0)
    c = lax.broadcasted_iota(jnp.int32, (ts, ts), 1)
    return jnp.where((r // CHUNK == c // CHUNK) & (c <= r), 1.0, 0.0).astype(dtype)


def _causal_conv(buf_ref, pre, conv_ref, ts):
    buf_ref[8:8 + ts, :] = pre
    acc = buf_ref[pl.ds(8 - (CONV_K - 1), ts), :] * conv_ref[0:1, :]
    for k in range(1, CONV_K):
        acc = acc + buf_ref[pl.ds(8 - (CONV_K - 1) + k, ts), :] * conv_ref[k:k + 1, :]
    buf_ref[8 - (CONV_K - 1):8, :] = buf_ref[8 + ts - (CONV_K - 1):8 + ts, :]
    return acc


def _mixer_a_kernel(x_ref, w_ref, conv_ref, alog_ref, dtb_ref, norm_ref, y_ref, buf_ref, state_ref, *, ts):
    nct = ts // CHUNK

    @pl.when(pl.program_id(1) == 0)
    def _():
        buf_ref[0:8, :] = jnp.zeros((8, A_QKV_W), F32)
        state_ref[...] = jnp.zeros_like(state_ref)

    xb = x_ref[0].astype(BF16)
    proj = _dot(xb, w_ref[...])
    qkv = _silu(_causal_conv(buf_ref, proj[:, :A_QKV_W], conv_ref, ts))
    z = proj[:, A_QKV_W:A_QKV_W + A_V_W]
    a_raw = proj[:, 2048:2048 + LANES]
    b_raw = proj[:, 2048 + LANES:2048 + 2 * LANES]
    beta = jax.nn.sigmoid(b_raw)
    g = -jnp.exp(alog_ref[...]) * _softplus(a_raw + dtb_ref[...])
    gc = _dot_sel_lhs(_chunk_tri(ts, BF16), g)
    gc3 = gc.reshape(nct, CHUNK, LANES)
    beta3 = beta.reshape(nct, CHUNK, LANES)
    gct3 = jnp.swapaxes(gc3, 1, 2)
    ri = lax.broadcasted_iota(jnp.int32, (CHUNK, CHUNK), 0)
    ci = lax.broadcasted_iota(jnp.int32, (CHUNK, CHUNK), 1)
    strict = (ci < ri)[None]
    eye = jnp.where(ri == ci, 1.0, 0.0).astype(F32)[None]
    nh = A_HEADS

    def stack(per_head):
        parts = [per_head(h) for h in range(nh)]
        return jnp.concatenate([parts[h][c:c + 1] for c in range(nct) for h in range(nh)], axis=0)

    q = stack(lambda h: qkv[:, h * A_DK:(h + 1) * A_DK].reshape(nct, CHUNK, A_DK))
    k = stack(lambda h: qkv[:, (nh + h) * A_DK:(nh + h + 1) * A_DK].reshape(nct, CHUNK, A_DK))
    v = stack(lambda h: qkv[:, 2 * nh * A_DK + h * A_DV:2 * nh * A_DK + (h + 1) * A_DV].reshape(nct, CHUNK, A_DV))
    zs = stack(lambda h: z[:, h * A_DV:(h + 1) * A_DV].reshape(nct, CHUNK, A_DV))
    gcol = stack(lambda h: gc3[:, :, h:h + 1])
    grow = stack(lambda h: gct3[:, h:h + 1, :])
    bcol = stack(lambda h: beta3[:, :, h:h + 1])
    glast = stack(lambda h: gc3[:, CHUNK - 1:CHUNK, h:h + 1])
    q = q * lax.rsqrt(jnp.sum(q * q, axis=-1, keepdims=True) + EPS) * (A_DK ** -0.5)
    k = k * lax.rsqrt(jnp.sum(k * k, axis=-1, keepdims=True) + EPS)
    decay = jnp.where(strict, jnp.exp(jnp.where(strict, gcol - grow, 0.0)), 0.0)
    kb = k.astype(BF16)
    amat = bcol * jnp.einsum('bid,bjd->bij', kb, kb, preferred_element_type=F32) * decay
    xinv = eye - amat
    pw = amat.astype(BF16)
    for _ in range(5):
        pw = _bmm(pw, pw).astype(BF16)
        xinv = xinv + _bmm(xinv.astype(BF16), pw)
    ah, al = _split2(amat)
    xh, xl = _split2(xinv)
    resid = eye - xinv - (_bmm(ah, xh) + _bmm(ah, xl) + _bmm(al, xh))
    xinv = xinv + _bmm(xh, resid.astype(BF16))
    rhs = jnp.concatenate([v * bcol, k * (bcol * jnp.exp(gcol))], axis=-1)
    sol = _bmm(xinv.astype(BF16), rhs.astype(BF16))
    u = sol[:, :, :A_DV]
    wq = jnp.concatenate([sol[:, :, A_DV:], q], axis=1).astype(BF16)
    kend = (k * jnp.exp(glast - gcol)).astype(BF16)
    qk = jnp.einsum('bid,bjd->bij', q.astype(BF16), kend, preferred_element_type=F32).astype(BF16)
    kend_t = jnp.swapaxes(kend, 1, 2)
    dec = jnp.exp(glast)
    s = state_ref[...]
    outs = []
    for c in range(nct):
        sl = slice(c * nh, (c + 1) * nh)
        r = _bmm(wq[sl], s.astype(BF16))
        delta = (u[sl] - r[:, :CHUNK]).astype(BF16)
        outs.append(dec[sl] * r[:, CHUNK:] + _bmm(qk[sl], delta))
        s = dec[sl] * s + _bmm(kend_t[sl], delta)
    state_ref[...] = s
    o = jnp.concatenate(outs, axis=0)
    o = o * lax.rsqrt(jnp.mean(o * o, axis=-1, keepdims=True) + EPS) * norm_ref[...]
    o = (o * _silu(zs)).astype(y_ref.dtype)
    for c in range(nct):
        for h in range(nh):
            y_ref[0, c * CHUNK:(c + 1) * CHUNK, h * A_DV:(h + 1) * A_DV] = o[c * nh + h]


def _mixer_a(x, w, conv, alog, dtb, norm, ts):
    bsz, s, d = x.shape
    wcols = w.shape[1]
    full = lambda shape: pl.BlockSpec(shape, lambda b, j: (0,) * len(shape))
    return pl.pallas_call(
        functools.partial(_mixer_a_kernel, ts=ts),
        grid=(bsz, s // ts),
        in_specs=[pl.BlockSpec((1, ts, d), lambda b, j: (b, j, 0)),
                  full((d, wcols)), full((CONV_K, A_QKV_W)), full((1, LANES)), full((1, LANES)),
                  full((1, A_DV))],
        out_specs=pl.BlockSpec((1, ts, A_V_W), lambda b, j: (b, j, 0)),
        out_shape=jax.ShapeDtypeStruct((bsz, s, A_V_W), BF16),
        scratch_shapes=[pltpu.VMEM((8 + ts, A_QKV_W), F32), pltpu.VMEM((A_HEADS, A_DK, A_DV), F32)],
        compiler_params=pltpu.CompilerParams(dimension_semantics=("parallel", "arbitrary"),
                                             vmem_limit_bytes=VMEM_LIMIT),
        name="mixer_a",
    )(x, w, conv, alog, dtb, norm)


def _mixer_b_kernel(x_ref, w_ref, cos_ref, sin_ref, intra_ref, read_ref, write_ref, cdec_ref, norm_ref,
                    y_ref, state_ref, *, ts):
    nct = ts // CHUNK

    @pl.when(pl.program_id(1) == 0)
    def _():
        state_ref[...] = jnp.zeros_like(state_ref)

    xb = x_ref[0].astype(BF16)
    proj = _dot(xb, w_ref[...])
    cos = cos_ref[...]
    sin = sin_ref[...]
    nh = B_HEADS

    def stack(per_head):
        parts = [per_head(h) for h in range(nh)]
        return jnp.concatenate([parts[h][c:c + 1] for c in range(nct) for h in range(nh)], axis=0)

    def rope(t):
        return t * cos + pltpu.roll(t, B_DK // 2, 1) * sin

    q = stack(lambda h: rope(proj[:, h * B_DK:(h + 1) * B_DK]).reshape(nct, CHUNK, B_DK)).astype(BF16)
    k = stack(lambda h: (rope(proj[:, B_W + h * B_DK:B_W + (h + 1) * B_DK]) * (B_DK ** -0.5))
              .reshape(nct, CHUNK, B_DK))
    v = stack(lambda h: proj[:, 2 * B_W + h * B_DV:2 * B_W + (h + 1) * B_DV].reshape(nct, CHUNK, B_DV)).astype(BF16)
    gate = stack(lambda h: proj[:, 3 * B_W + h * B_DV:3 * B_W + (h + 1) * B_DV].reshape(nct, CHUNK, B_DV))
    tile_heads = lambda ref: jnp.concatenate([ref[...]] * nct, axis=0)
    scores = jnp.einsum('bid,bjd->bij', q, k.astype(BF16), preferred_element_type=F32) * tile_heads(intra_ref)
    o = _bmm(scores.astype(BF16), v)
    kw_t = jnp.swapaxes((k * tile_heads(write_ref)).astype(BF16), 1, 2)
    kv = _bmm(kw_t, v)
    s = state_ref[...]
    cdec = cdec_ref[...]
    prev = []
    for c in range(nct):
        prev.append(s.astype(BF16))
        s = cdec * s + kv[c * nh:(c + 1) * nh]
    state_ref[...] = s
    o = o + _bmm(q, jnp.concatenate(prev, axis=0)) * tile_heads(read_ref)
    mu = jnp.mean(o, axis=-1, keepdims=True)
    oc = o - mu
    var = jnp.mean(oc * oc, axis=-1, keepdims=True)
    norm = jnp.concatenate([norm_ref[:, h * B_DV:(h + 1) * B_DV][None] for h in range(nh)] * nct, axis=0)
    o = (oc * lax.rsqrt(var + EPS) * norm * _silu(gate)).astype(y_ref.dtype)
    for c in range(nct):
        for h in range(nh):
            y_ref[0, c * CHUNK:(c + 1) * CHUNK, h * B_DV:(h + 1) * B_DV] = o[c * nh + h]


def _retention_tables():
    log_gamma = jnp.log1p(-jnp.exp2(-5.0 - jnp.arange(B_HEADS, dtype=F32)))
    idx = jnp.arange(CHUNK, dtype=F32)
    intra = jnp.exp(log_gamma[:, None, None] * jnp.abs(idx[:, None] - idx[None, :]))
    read = jnp.broadcast_to(jnp.exp(log_gamma[:, None] * (idx + 1.0))[:, :, None], (B_HEADS, CHUNK, B_DV))
    write = jnp.broadcast_to(jnp.exp(log_gamma[:, None] * (CHUNK - 1.0 - idx))[:, :, None], (B_HEADS, CHUNK, B_DK))
    cdec = jnp.broadcast_to(jnp.exp(log_gamma * CHUNK)[:, None, None], (B_HEADS, 1, B_DV))
    return intra, read, write, cdec


def _rope_tables(s):
    pos = jnp.arange(s, dtype=F32)
    inv_freq = ROPE_BASE ** (-jnp.arange(0, B_DK, 2, dtype=F32) / B_DK)
    ang = pos[:, None] * inv_freq[None, :]
    cos, sin = jnp.cos(ang), jnp.sin(ang)
    return jnp.concatenate([cos, cos], axis=-1), jnp.concatenate([-sin, sin], axis=-1)


def _mixer_b(x, w, cos2, sin2, tables, norm, ts):
    bsz, s, d = x.shape
    intra, read, write, cdec = tables
    full = lambda shape: pl.BlockSpec(shape, lambda b, j: (0,) * len(shape))
    return pl.pallas_call(
        functools.partial(_mixer_b_kernel, ts=ts),
        grid=(bsz, s // ts),
        in_specs=[pl.BlockSpec((1, ts, d), lambda b, j: (b, j, 0)),
                  full((d, 4 * B_W)),
                  pl.BlockSpec((ts, B_DK), lambda b, j: (j, 0)), pl.BlockSpec((ts, B_DK), lambda b, j: (j, 0)),
                  full(intra.shape), full(read.shape), full(write.shape), full(cdec.shape), full((1, B_W))],
        out_specs=pl.BlockSpec((1, ts, B_W), lambda b, j: (b, j, 0)),
        out_shape=jax.ShapeDtypeStruct((bsz, s, B_W), BF16),
        scratch_shapes=[pltpu.VMEM((B_HEADS, B_DK, B_DV), F32)],
        compiler_params=pltpu.CompilerParams(dimension_semantics=("parallel", "arbitrary"),
                                             vmem_limit_bytes=VMEM_LIMIT),
        name="mixer_b",
    )(x, w, cos2, sin2, intra, read, write, cdec, norm)


def _mixer_c_kernel(x_ref, w_ref, conv_ref, convb_ref, dtb_ref, alog_ref, dskip_ref, norm_ref, expand_ref,
                    y_ref, buf_ref, state_ref, *, ts):
    nct = ts // CHUNK

    @pl.when(pl.program_id(1) == 0)
    def _():
        buf_ref[0:8, :] = jnp.zeros((8, C_XBC_W), F32)
        state_ref[...] = jnp.zeros_like(state_ref)

    xb = x_ref[0].astype(BF16)
    proj = _dot(xb, w_ref[...])
    z = proj[:, :C_INNER]
    xbc = _silu(_causal_conv(buf_ref, proj[:, C_INNER:C_INNER + C_XBC_W], conv_ref, ts) + convb_ref[...])
    xs = xbc[:, :C_INNER]
    bm = xbc[:, C_INNER:C_INNER + C_GROUPS * C_STATE]
    cm = xbc[:, C_INNER + C_GROUPS * C_STATE:]
    dt = _softplus(proj[:, C_INNER + C_XBC_W:] + dtb_ref[...])
    lstep = dt * (-jnp.exp(alog_ref[...]))
    lc = _dot_sel_lhs(_chunk_tri(ts, BF16), lstep)
    expand = expand_ref[...]
    lce = _dot_sel_rhs(lc, expand)
    dte = _dot(dt.astype(BF16), expand)
    ng = C_GROUPS

    def stack(per_group):
        parts = [per_group(g) for g in range(ng)]
        return jnp.concatenate([parts[g][c:c + 1] for c in range(nct) for g in range(ng)], axis=0)

    def group_lanes(a, width=C_GROUP_W):
        return stack(lambda g: a[:, g * width:(g + 1) * width].reshape(nct, CHUNK, width))

    def group_rows(ref):
        return jnp.concatenate([ref[:, g * C_GROUP_W:(g + 1) * C_GROUP_W][None] for g in range(ng)] * nct, axis=0)

    lg = group_lanes(lce)
    dg = group_lanes(dte)
    xg = group_lanes(xs)
    zg = group_lanes(z)
    bmg = group_lanes(bm, C_STATE).astype(BF16)
    cmg = group_lanes(cm, C_STATE).astype(BF16)
    heads_per_group = C_HEADS // ng

    def row_form(a):
        at = jnp.swapaxes(a.reshape(nct, CHUNK, LANES), 1, 2)
        return stack(lambda g: jnp.concatenate(
            [at[:, g * heads_per_group + h:g * heads_per_group + h + 1, :] for h in range(heads_per_group)], axis=-1))

    lc_row = row_form(lc)
    dt_row = row_form(dt)
    bm_t = jnp.concatenate([bmg] * (C_GROUP_W // CHUNK), axis=1)
    cbt = jnp.einsum('bin,bmn->bim', cmg, bm_t, preferred_element_type=F32)
    wm = (cbt * jnp.exp(-jnp.abs(lg - lc_row)) * dt_row).astype(BF16)
    lane = lax.broadcasted_iota(jnp.int32, (1, 1, LANES), 2)
    left = lane < C_HEAD_DIM
    pair_out = []
    for p in range(C_GROUP_W // LANES):
        xp = xg[:, :, p * LANES:(p + 1) * LANES]
        bd = jnp.concatenate([jnp.where(left, xp, 0.0), jnp.where(left, 0.0, xp)], axis=1)
        pair_out.append(_bmm(wm[:, :, p * LANES:(p + 1) * LANES], bd.astype(BF16)))
    y = jnp.concatenate(pair_out, axis=-1)
    llast = lg[:, CHUNK - 1:CHUNK, :]
    xw = (xg * (jnp.exp(llast - lg) * dg)).astype(BF16)
    kv = _bmm(jnp.swapaxes(bmg, 1, 2), xw)
    sdec = jnp.exp(llast)
    st = state_ref[...]
    prev = []
    for ch in range(nct):
        sl = slice(ch * ng, (ch + 1) * ng)
        prev.append(st.astype(BF16))
        st = sdec[sl] * st + kv[sl]
    state_ref[...] = st
    y = y + _bmm(cmg, jnp.concatenate(prev, axis=0)) * jnp.exp(lg)
    y = (y + group_rows(dskip_ref) * xg) * _silu(zg)
    y = (y * lax.rsqrt(jnp.mean(y * y, axis=-1, keepdims=True) + EPS) * group_rows(norm_ref)).astype(y_ref.dtype)
    for ch in range(nct):
        for g in range(ng):
            y_ref[0, ch * CHUNK:(ch + 1) * CHUNK, g * C_GROUP_W:(g + 1) * C_GROUP_W] = y[ch * ng + g]


def _mixer_c(x, w, conv, convb, dtb, alog, dskip, norm, expand, ts):
    bsz, s, d = x.shape
    wcols = w.shape[1]
    full = lambda shape: pl.BlockSpec(shape, lambda b, j: (0,) * len(shape))
    return pl.pallas_call(
        functools.partial(_mixer_c_kernel, ts=ts),
        grid=(bsz, s // ts),
        in_specs=[pl.BlockSpec((1, ts, d), lambda b, j: (b, j, 0)),
                  full((d, wcols)), full((CONV_K, C_XBC_W)), full((1, C_XBC_W)), full((1, LANES)),
                  full((1, LANES)), full((1, C_INNER)), full((1, C_INNER)), full((LANES, C_INNER))],
        out_specs=pl.BlockSpec((1, ts, C_INNER), lambda b, j: (b, j, 0)),
        out_shape=jax.ShapeDtypeStruct((bsz, s, C_INNER), BF16),
        scratch_shapes=[pltpu.VMEM((8 + ts, C_XBC_W), F32), pltpu.VMEM((C_GROUPS, C_STATE, C_GROUP_W), F32)],
        compiler_params=pltpu.CompilerParams(dimension_semantics=("parallel", "arbitrary"),
                                             vmem_limit_bytes=VMEM_LIMIT),
        name="mixer_c",
    )(x, w, conv, convb, dtb, alog, dskip, norm, expand)


def _layernorm(h, g, b):
    mu = jnp.mean(h, axis=-1, keepdims=True)
    hc = h - mu
    var = jnp.mean(hc * hc, axis=-1, keepdims=True)
    return hc * lax.rsqrt(var + EPS) * g + b


def _merge_kernel(x_ref, ya_ref, yb_ref, yc_ref, wg_ref, bg_ref, wa_ref, wb_ref, wc_ref, wo_ref,
                  lng_ref, lnb_ref, wrh_ref, wrl_ref, br_ref, x1_ref, x1b_ref, ids_ref, wts_ref, cnt_ref,
                  run_ref):
    @pl.when(pl.program_id(0) == 0)
    def _():
        run_ref[...] = jnp.zeros_like(run_ref)

    x = x_ref[...]
    xb = x.astype(BF16)
    merged = None
    for i, (y_ref, w_ref) in enumerate(((ya_ref, wa_ref), (yb_ref, wb_ref), (yc_ref, wc_ref))):
        gate = jax.nn.sigmoid(_dot(xb, wg_ref[:, i * D_MODEL:(i + 1) * D_MODEL]) + bg_ref[i:i + 1, :])
        term = gate * _dot(y_ref[...], w_ref[...])
        merged = term if merged is None else merged + term
    mix = _dot(merged.astype(BF16), wo_ref[...])
    x1 = _layernorm(DEEPNORM_ALPHA * x + mix, lng_ref[...], lnb_ref[...])
    x1_ref[...] = x1
    x1b_ref[...] = x1.astype(BF16)

    xh, xl = _split2(x1)
    logits = _dot(xh, wrh_ref[...]) + _dot(xl, wrh_ref[...]) + _dot(xh, wrl_ref[...]) + br_ref[...]
    tm = logits.shape[0]
    lane = lax.broadcasted_iota(jnp.int32, logits.shape, 1)
    neg = jnp.float32(-jnp.inf)
    gmask = lane < N_GROUPS
    gl = jnp.where(gmask, logits, neg)
    gmax = jnp.max(gl, axis=-1, keepdims=True)
    gidx = jnp.min(jnp.where(gmask & (gl == gmax), lane, LANES), axis=-1, keepdims=True)
    gp = 1.0 / jnp.sum(jnp.where(gmask, jnp.exp(gl - gmax), 0.0), axis=-1, keepdims=True)
    emask = (lane >= N_GROUPS) & (lane < N_GROUPS + N_EXPERTS) & ((lane - N_GROUPS) // EXPERTS_PER_GROUP == gidx)
    el = jnp.where(emask, logits, neg)
    v1 = jnp.max(el, axis=-1, keepdims=True)
    i1 = jnp.min(jnp.where(emask & (el == v1), lane, LANES), axis=-1, keepdims=True)
    el2 = jnp.where(lane == i1, neg, el)
    v2 = jnp.max(el2, axis=-1, keepdims=True)
    i2 = jnp.min(jnp.where(emask & (lane != i1) & (el2 == v2), lane, LANES), axis=-1, keepdims=True)
    e21 = jnp.exp(v2 - v1)
    w1 = gp / (1.0 + e21)
    w2 = gp * e21 / (1.0 + e21)

    hot1 = lane == i1
    hot2 = lane == i2
    hot = jnp.where(hot1 | hot2, 1.0, 0.0)
    r = lax.broadcasted_iota(jnp.int32, (tm, tm), 0)
    c = lax.broadcasted_iota(jnp.int32, (tm, tm), 1)
    before = _dot(jnp.where(c < r, 1.0, 0.0).astype(BF16), hot.astype(BF16)) + run_ref[0:1, :]
    rank1 = jnp.sum(jnp.where(hot1, before, 0.0), axis=-1, keepdims=True)
    rank2 = jnp.sum(jnp.where(hot2, before, 0.0), axis=-1, keepdims=True)
    run_ref[0:1, :] = run_ref[0:1, :] + jnp.sum(hot, axis=0, keepdims=True)
    cnt_ref[...] = jnp.broadcast_to(run_ref[0:1, :], cnt_ref.shape)

    ids = jnp.where(lane == 0, (i1 - N_GROUPS).astype(F32), jnp.where(lane == 1, (i2 - N_GROUPS).astype(F32),
                    jnp.where(lane == 2, rank1, jnp.where(lane == 3, rank2, 0.0))))
    ids_ref[...] = jnp.transpose(ids)[0:8, :].astype(jnp.int32)
    wts_ref[...] = jnp.where(lane == 0, w1, jnp.where(lane == 1, w2, 0.0))


def _merge(x2d, ya, yb, yc, wg, bg, wa, wb, wc, wo, lng, lnb, wrh, wrl, br, tm):
    t, d = x2d.shape
    full = lambda shape: pl.BlockSpec(shape, lambda i: (0,) * len(shape))
    row = lambda w: pl.BlockSpec((tm, w), lambda i: (i, 0))
    return pl.pallas_call(
        _merge_kernel,
        grid=(t // tm,),
        in_specs=[row(d), row(A_V_W), row(B_W), row(C_INNER),
                  full(wg.shape), full(bg.shape), full(wa.shape), full(wb.shape), full(wc.shape), full(wo.shape),
                  full((1, d)), full((1, d)), full(wrh.shape), full(wrl.shape), full((1, LANES))],
        out_specs=[row(d), row(d), pl.BlockSpec((8, tm), lambda i: (0, i)), row(LANES), full((8, LANES))],
        out_shape=[jax.ShapeDtypeStruct((t, d), F32), jax.ShapeDtypeStruct((t, d), BF16),
                   jax.ShapeDtypeStruct((8, t), jnp.int32), jax.ShapeDtypeStruct((t, LANES), F32),
                   jax.ShapeDtypeStruct((8, LANES), F32)],
        scratch_shapes=[pltpu.VMEM((8, LANES), F32)],
        compiler_params=pltpu.CompilerParams(dimension_semantics=("arbitrary",), vmem_limit_bytes=VMEM_LIMIT),
        name="merge_ln_router",
    )(x2d, ya, yb, yc, wg, bg, wa, wb, wc, wo, lng, lnb, wrh, wrl, br)


def _experts_kernel(te_ref, nv_ref, xs_ref, wg_ref, wu_ref, wd_ref, *rest, first_tile):
    ys_ref, wgb_ref, wub_ref, wdb_ref = rest[-4:]
    i = pl.program_id(0)
    tile = i + first_tile

    @pl.when((i == 0) | (te_ref[tile] != te_ref[jnp.maximum(tile - 1, 0)]))
    def _():
        wgb_ref[...] = wg_ref[0, 0].astype(BF16)
        wub_ref[...] = wu_ref[0, 0].astype(BF16)
        wdb_ref[...] = wd_ref[0, 0].astype(BF16)

    @pl.when(tile < nv_ref[0])
    def _():
        xs = xs_ref[...]
        h = _silu(_dot(xs, wgb_ref[...])) * _dot(xs, wub_ref[...])
        ys_ref[...] = _dot(h.astype(BF16), wdb_ref[...]).astype(ys_ref.dtype)

    @pl.when(tile >= nv_ref[0])
    def _():
        ys_ref[...] = jnp.zeros_like(ys_ref)


def _experts(tile_expert, n_valid, xs, wg, wu, wd, layer, tm, first_tile, total_rows, ys_prev):
    rows, d = xs.shape
    wspec = lambda shape: pl.BlockSpec((1, 1) + shape, lambda i, te, nv: (layer, te[i + first_tile], 0, 0))
    in_specs = [pl.BlockSpec((tm, d), lambda i, te, nv: (i, 0)),
                wspec((d, D_EXPERT)), wspec((d, D_EXPERT)), wspec((D_EXPERT, d))]
    args = [tile_expert, n_valid, xs, wg, wu, wd]
    aliases = {}
    if ys_prev is not None:
        in_specs.append(pl.BlockSpec(memory_space=pl.ANY))
        aliases = {len(args): 0}
        args.append(ys_prev)
    grid_spec = pltpu.PrefetchScalarGridSpec(
        num_scalar_prefetch=2,
        grid=(rows // tm,),
        in_specs=in_specs,
        out_specs=pl.BlockSpec((tm, d), lambda i, te, nv: (i + first_tile, 0)),
        scratch_shapes=[pltpu.VMEM((d, D_EXPERT), BF16), pltpu.VMEM((d, D_EXPERT), BF16),
                        pltpu.VMEM((D_EXPERT, d), BF16)],
    )
    return pl.pallas_call(
        functools.partial(_experts_kernel, first_tile=first_tile),
        grid_spec=grid_spec,
        out_shape=jax.ShapeDtypeStruct((total_rows, d), BF16),
        input_output_aliases=aliases,
        compiler_params=pltpu.CompilerParams(dimension_semantics=("arbitrary",), vmem_limit_bytes=VMEM_LIMIT),
        name="grouped_experts",
    )(*args)


def _combine_kernel(x_ref, g_ref, wts_ref, lng_ref, lnb_ref, *rest):
    o_ref = rest[-1]
    wts = wts_ref[...]
    ffn = wts[:, 0:1] * g_ref[0].astype(F32) + wts[:, 1:2] * g_ref[1].astype(F32)
    o_ref[...] = _layernorm(DEEPNORM_ALPHA * x_ref[...] + ffn, lng_ref[...], lnb_ref[...])


def _combine(x1, g, wts, lng, lnb, tm, first_tile, out_prev):
    t, d = x1.shape
    row = lambda w: pl.BlockSpec((tm, w), lambda i: (i + first_tile, 0))
    vec = pl.BlockSpec((1, d), lambda i: (0, 0))
    in_specs = [row(d), pl.BlockSpec((2, tm, d), lambda i: (0, i, 0)), row(LANES), vec, vec]
    args = [x1, g, wts, lng, lnb]
    aliases = {}
    if out_prev is not None:
        in_specs.append(pl.BlockSpec(memory_space=pl.ANY))
        aliases = {len(args): 0}
        args.append(out_prev)
    return pl.pallas_call(
        _combine_kernel,
        grid=(g.shape[1] // tm,),
        in_specs=in_specs,
        out_specs=row(d),
        out_shape=jax.ShapeDtypeStruct((t, d), F32),
        input_output_aliases=aliases,
        compiler_params=pltpu.CompilerParams(dimension_semantics=("parallel",), vmem_limit_bytes=VMEM_LIMIT),
        name="combine_ln",
    )(*args)


def _pad_lanes(v, width=LANES):
    v = v.reshape(1, -1).astype(F32)
    return jnp.pad(v, ((0, 0), (0, width - v.shape[1])))


def _rows(a, idx):
    return a.at[idx].get(mode="promise_in_bounds")


def _dispatch(ids, counts, t, tm):
    n_pairs = 2 * t
    n_tiles = n_pairs // tm + N_EXPERTS
    e1, e2, rank1, rank2 = ids[0], ids[1], ids[2], ids[3]
    padded = ((counts + tm - 1) // tm) * tm
    pend = jnp.cumsum(padded)
    pstart = pend - padded
    start = jnp.cumsum(counts) - counts
    pos = jnp.concatenate([_rows(pstart, e1) + rank1, _rows(pstart, e2) + rank2])
    tok2 = 2 * jnp.arange(t, dtype=jnp.int32)
    key = jnp.concatenate([e1 * n_pairs + tok2, e2 * n_pairs + tok2 + 1])
    order = lax.sort(key) % n_pairs
    n_valid = (pend[-1] // tm).astype(jnp.int32).reshape(1)
    tile_start = jnp.arange(n_tiles, dtype=jnp.int32) * tm
    count_le = lambda v: jnp.sum((pend[None, :] <= v[:, None]).astype(jnp.int32), axis=1)
    tile_e = jnp.minimum(count_le(tile_start), count_le(pend[-1:] - 1))
    row = jnp.arange(n_tiles * tm, dtype=jnp.int32)
    row_e = jnp.repeat(tile_e, tm)
    off = row - _rows(pstart, row_e)
    valid = (off >= 0) & (off < _rows(counts, row_e))
    idx = jnp.where(valid, _rows(start, row_e) + off, 0)
    row_tok = jnp.where(valid, _rows(order, idx) // 2, row % t)
    return row_tok, pos, tile_e, n_valid


def _layer(x, p, consts, seq_tile, token_tile, expert_tile):
    bsz, s, d = x.shape
    t = bsz * s
    ya = _mixer_a(x, p["w_a"], p["conv_a"], p["a_log_a"], p["dt_bias_a"], p["norm_a"], seq_tile)
    yb = _mixer_b(x, p["w_b"], consts["cos2"], consts["sin2"], consts["ret"], p["norm_b"], seq_tile)
    yc = _mixer_c(x, p["w_c"], p["conv_c"], p["conv_bias_c"], p["dt_bias_c"], p["a_log_c"], p["d_skip_c"],
                  p["norm_c"], consts["expand"], seq_tile)
    x1, x1b, ids, wts, cnt = _merge(x.reshape(t, d), ya.reshape(t, -1), yb.reshape(t, -1), yc.reshape(t, -1),
                                    p["w_gate_in"], p["b_gate"], p["w_branch_a"], p["w_branch_b"],
                                    p["w_branch_c"], p["w_out"], p["ln1_g"], p["ln1_b"], p["w_router_hi"],
                                    p["w_router_lo"], p["b_router"], token_tile)
    counts = cnt[0, N_GROUPS:N_GROUPS + N_EXPERTS].astype(jnp.int32)
    row_tok, pos, tile_e, n_valid = _dispatch(ids, counts, t, expert_tile)
    n_rows = row_tok.shape[0]
    rows_c = n_rows // DISPATCH_CHUNKS
    ys = None
    for k in range(DISPATCH_CHUNKS):
        xs = _rows(x1b, row_tok[k * rows_c:(k + 1) * rows_c])
        ys = _experts(tile_e, n_valid, xs, p["w_gate_e"], p["w_up_e"], p["w_down_e"], p["layer"], expert_tile,
                      k * (rows_c // expert_tile), n_rows, ys)
    pos = pos.reshape(2, t)
    t_c = t // DISPATCH_CHUNKS
    x2 = None
    for k in range(DISPATCH_CHUNKS):
        g = _rows(ys, pos[:, k * t_c:(k + 1) * t_c].reshape(-1)).reshape(2, t_c, d)
        x2 = _combine(x1, g, wts, p["ln2_g"], p["ln2_b"], token_tile, k * (t_c // token_tile), x2)
    return x2.reshape(bsz, s, d)


def _layer_params(l, w_in, conv_a, a_log_a, dt_bias_a, norm_a, norm_b, conv_c, conv_bias_c, dt_bias_c,
                  a_log_c, d_skip_c, norm_c, b_gate, w_branch_a, w_branch_b, w_branch_c, w_out,
                  ln1_g, ln1_b, w_router_group, b_router_group, w_router_expert, b_router_expert,
                  w_gate_e, w_up_e, w_down_e, ln2_g, ln2_b):
    o = IN_OFFS
    w = w_in[l]
    d = w.shape[0]
    zpad = lambda n: jnp.zeros((d, n), F32)
    w_a = jnp.concatenate([w[:, o[0]:o[2]], w[:, o[2]:o[3]], zpad(LANES - A_HEADS),
                           w[:, o[3]:o[4]], zpad(LANES - A_HEADS)], axis=1).astype(BF16)
    w_b = w[:, o[4]:o[8]].astype(BF16)
    w_c = jnp.concatenate([w[:, o[8]:o[10]], w[:, o[10]:o[11]], zpad(LANES - C_HEADS)], axis=1).astype(BF16)
    w_router = jnp.concatenate([w_router_group[l], w_router_expert[l],
                                zpad(LANES - N_GROUPS - N_EXPERTS)], axis=1)
    b_router = _pad_lanes(jnp.concatenate([b_router_group[l], b_router_expert[l]]))
    return {
        "w_a": w_a, "w_b": w_b, "w_c": w_c, "w_gate_in": w[:, o[11]:o[12]].astype(BF16),
        "conv_a": conv_a[l], "a_log_a": _pad_lanes(a_log_a[l]), "dt_bias_a": _pad_lanes(dt_bias_a[l]),
        "norm_a": norm_a[l].reshape(1, -1), "norm_b": norm_b[l].reshape(1, -1),
        "conv_c": conv_c[l], "conv_bias_c": conv_bias_c[l].reshape(1, -1),
        "dt_bias_c": _pad_lanes(dt_bias_c[l]), "a_log_c": _pad_lanes(a_log_c[l]),
        "d_skip_c": jnp.repeat(d_skip_c[l], C_HEAD_DIM).reshape(1, -1), "norm_c": norm_c[l].reshape(1, -1),
        "b_gate": b_gate[l],
        "w_branch_a": w_branch_a[l].astype(BF16), "w_branch_b": w_branch_b[l].astype(BF16),
        "w_branch_c": w_branch_c[l].astype(BF16), "w_out": w_out[l].astype(BF16),
        "ln1_g": ln1_g[l].reshape(1, -1), "ln1_b": ln1_b[l].reshape(1, -1),
        "w_router_hi": w_router.astype(BF16),
        "w_router_lo": (w_router - w_router.astype(BF16).astype(F32)).astype(BF16), "b_router": b_router,
        "w_gate_e": w_gate_e, "w_up_e": w_up_e, "w_down_e": w_down_e, "layer": l,
        "ln2_g": ln2_g[l].reshape(1, -1), "ln2_b": ln2_b[l].reshape(1, -1),
    }


def _forward(x, params, seq_tile=SEQ_TILE, token_tile=TOKEN_TILE, expert_tile=EXPERT_TILE):
    s = x.shape[1]
    cos2, sin2 = _rope_tables(s)
    head_of_lane = jnp.arange(C_INNER) // C_HEAD_DIM
    expand = (jnp.arange(LANES)[:, None] == head_of_lane[None, :]).astype(BF16)
    consts = {"cos2": cos2, "sin2": sin2, "ret": _retention_tables(), "expand": expand}
    for l in range(DEPTH):
        x = _layer(x, _layer_params(l, *params), consts, seq_tile, token_tile, expert_tile)
    return x


def kernel(x, w_in, conv_a, a_log_a, dt_bias_a, norm_a, norm_b, conv_c, conv_bias_c, dt_bias_c, a_log_c, d_skip_c, norm_c, b_gate, w_branch_a, w_branch_b, w_branch_c, w_out, ln1_g, ln1_b, w_router_group, b_router_group, w_router_expert, b_router_expert, w_gate_e, w_up_e, w_down_e, ln2_g, ln2_b):
    params = (w_in, conv_a, a_log_a, dt_bias_a, norm_a, norm_b, conv_c, conv_bias_c, dt_bias_c, a_log_c,
              d_skip_c, norm_c, b_gate, w_branch_a, w_branch_b, w_branch_c, w_out, ln1_g, ln1_b,
              w_router_group, b_router_group, w_router_expert, b_router_expert, w_gate_e, w_up_e, w_down_e,
              ln2_g, ln2_b)
    return _forward(x, params)
```

```python
import functools
import math

import jax
import jax.numpy as jnp
from jax import lax
from jax.experimental import pallas as pl
from jax.experimental.pallas import tpu as pltpu

F32 = jnp.float32
BF16 = jnp.bfloat16

D_MODEL = 1024
DEPTH = 2
CHUNK = 64
CONV_K = 4
EPS = 1e-6
LANES = 128
A_HEADS = 4
A_DK = 128
A_DV = 128
A_QKV_W = 3 * A_HEADS * A_DK
A_V_W = A_HEADS * A_DV
B_HEADS = 4
B_DK = 128
B_DV = 128
B_W = B_HEADS * B_DK
ROPE_BASE = 10000.0
C_INNER = 1024
C_HEAD_DIM = 64
C_HEADS = 16
C_GROUPS = 2
C_STATE = 128
C_XBC_W = C_INNER + 2 * C_GROUPS * C_STATE
C_GROUP_W = C_INNER // C_GROUPS
N_BRANCH = 3
N_GROUPS = 4
EXPERTS_PER_GROUP = 8
N_EXPERTS = 32
D_EXPERT = 512
DEEPNORM_ALPHA = (2 * DEPTH) ** 0.25

IN_SIZES = (A_QKV_W, A_V_W, A_HEADS, A_HEADS, B_W, B_W, B_W, B_W, C_INNER, C_XBC_W, C_HEADS,
            N_BRANCH * D_MODEL)
IN_OFFS = tuple(int(sum(IN_SIZES[:i])) for i in range(len(IN_SIZES) + 1))

SEQ_TILE = 512
TOKEN_TILE = 512
EXPERT_TILE = 512
DISPATCH_CHUNKS = 4
VMEM_LIMIT = 56 * 1024 * 1024


def _dot(a, b):
    return jnp.dot(a, b, preferred_element_type=F32)


def _split2(a):
    hi = a.astype(BF16)
    return hi, (a - hi.astype(F32)).astype(BF16)


def _bmm(a, b):
    return jnp.einsum('cij,cjk->cik', a, b, preferred_element_type=F32)


def _dot_sel_rhs(a, sel):
    hi, lo = _split2(a)
    return _dot(hi, sel) + _dot(lo, sel)


def _sigmoid(x):
    return 0.5 * jnp.tanh(0.5 * x) + 0.5


def _silu(x):
    h = 0.5 * x
    return h + h * jnp.tanh(h)


def _softplus(x):
    return jnp.maximum(x, 0.0) + jnp.log(1.0 + jnp.exp(-jnp.abs(x)))


def _chunk_cumsum(a):
    ts, lanes = a.shape
    nct = ts // CHUNK
    r = lax.broadcasted_iota(jnp.int32, (nct, CHUNK, CHUNK), 1)
    c = lax.broadcasted_iota(jnp.int32, (nct, CHUNK, CHUNK), 2)
    tri = jnp.where(c <= r, 1.0, 0.0).astype(BF16)
    hi, lo = _split2(a.reshape(nct, CHUNK, lanes))
    return (_bmm(tri, hi) + _bmm(tri, lo)).reshape(ts, lanes)


def _causal_conv(buf_ref, pre, conv_ref, ts, c0, c1):
    buf_ref[8:8 + ts, c0:c1] = pre
    acc = buf_ref[pl.ds(8 - (CONV_K - 1), ts), c0:c1] * conv_ref[0:1, c0:c1]
    for k in range(1, CONV_K):
        acc = acc + buf_ref[pl.ds(8 - (CONV_K - 1) + k, ts), c0:c1] * conv_ref[k:k + 1, c0:c1]
    buf_ref[8 - (CONV_K - 1):8, c0:c1] = buf_ref[8 + ts - (CONV_K - 1):8 + ts, c0:c1]
    return acc


def _mixer_a_kernel(x_ref, w_ref, conv_ref, alog_ref, dtb_ref, norm_ref, e64_ref, y_ref, buf_ref, state_ref,
                    *, ts):
    nct = ts // CHUNK

    @pl.when(pl.program_id(1) == 0)
    def _():
        buf_ref[0:8, :] = jnp.zeros((8, A_QKV_W), F32)
        state_ref[...] = jnp.zeros_like(state_ref)

    xb = x_ref[0].astype(BF16)
    nh = A_HEADS
    qk_w = nh * A_DK

    def conv_cols(c0, c1):
        return _silu(_causal_conv(buf_ref, _dot(xb, w_ref[:, c0:c1]), conv_ref, ts, c0, c1))

    small = _dot(xb, w_ref[:, 2048:2048 + 2 * LANES])
    a_raw = small[:, :LANES]
    b_raw = small[:, LANES:]
    kf = conv_cols(qk_w, 2 * qk_w)
    beta = _sigmoid(b_raw)
    g = -jnp.exp(alog_ref[...]) * _softplus(a_raw + dtb_ref[...])
    gc = _chunk_cumsum(g)
    gc3 = gc.reshape(nct, CHUNK, LANES)
    beta3 = beta.reshape(nct, CHUNK, LANES)
    gct3 = jnp.swapaxes(gc3, 1, 2)
    wide = nh * CHUNK

    def stack(per_head):
        parts = [per_head(h) for h in range(nh)]
        return jnp.concatenate([parts[h][c:c + 1] for c in range(nct) for h in range(nh)], axis=0)

    def stack_heads(a, width):
        return stack(lambda h: a[:, h * width:(h + 1) * width].reshape(nct, CHUNK, width))

    k = stack_heads(kf, A_DK)
    gcol = stack(lambda h: gc3[:, :, h:h + 1])
    bcol = stack(lambda h: beta3[:, :, h:h + 1])
    glast = stack(lambda h: gc3[:, CHUNK - 1:CHUNK, h:h + 1])
    k = k * lax.rsqrt(jnp.sum(k * k, axis=-1, keepdims=True) + EPS)

    row_w = lax.broadcasted_iota(jnp.int32, (CHUNK, wide), 0)
    col_w = lax.broadcasted_iota(jnp.int32, (CHUNK, wide), 1)
    strict = (col_w % CHUNK < row_w)[None]
    eye = jnp.where(col_w % CHUNK == row_w, 1.0, 0.0).astype(F32)[None]
    blk_r = lax.broadcasted_iota(jnp.int32, (wide, wide), 0) // CHUNK
    blk_c = lax.broadcasted_iota(jnp.int32, (wide, wide), 1) // CHUNK
    same_head = (blk_r == blk_c)[None]

    def block_diag(m):
        tiled = jnp.concatenate([m] * nh, axis=1)
        return jnp.where(same_head, tiled, jnp.zeros_like(tiled))

    gcol_w = _dot_sel_rhs(gc, e64_ref[...]).reshape(nct, CHUNK, wide)
    beta_w = _dot(beta.astype(BF16), e64_ref[...]).reshape(nct, CHUNK, wide)
    grow_w = jnp.concatenate([gct3[:, h:h + 1, :] for h in range(nh)], axis=-1)
    decay = jnp.where(strict, jnp.exp(jnp.where(strict, gcol_w - grow_w, 0.0)), 0.0)
    kn = jnp.concatenate([k[c * nh:(c + 1) * nh].reshape(1, wide, A_DK) for c in range(nct)], axis=0)
    kn = kn.astype(BF16)
    k_nat = jnp.concatenate([jnp.concatenate([k[c * nh + h][None] for c in range(nct)], axis=0)
                             for h in range(nh)], axis=-1).astype(BF16)
    head_r = lax.broadcasted_iota(jnp.int32, (wide, qk_w), 0) // CHUNK
    head_c = lax.broadcasted_iota(jnp.int32, (wide, qk_w), 1) // A_DK
    k_bd = jnp.where((head_r == head_c)[None], jnp.concatenate([kn] * nh, axis=-1), jnp.zeros((), BF16))
    amat = beta_w * jnp.einsum('bid,bjd->bij', k_nat, k_bd, preferred_element_type=F32) * decay
    half = qk_w // 2
    side_work = [lambda: conv_cols(2 * qk_w, 2 * qk_w + half), lambda: conv_cols(2 * qk_w + half, 3 * qk_w),
                 lambda: conv_cols(0, half), lambda: conv_cols(half, qk_w),
                 lambda: _dot(xb, w_ref[:, A_QKV_W:A_QKV_W + A_V_W])]
    side = []
    xinv = eye - amat
    pw = amat.astype(BF16)
    pw_bd = block_diag(pw)
    for step in range(5):
        pw = _bmm(pw, pw_bd).astype(BF16)
        pw_bd = block_diag(pw)
        side.append(side_work[step]())
        xinv = xinv + _bmm(xinv.astype(BF16), pw_bd)
    ah, al = _split2(amat)
    xh, xl = _split2(xinv)
    xh_bd = block_diag(xh)
    resid = eye - xinv - (_bmm(ah, xh_bd) + _bmm(ah, block_diag(xl)) + _bmm(al, xh_bd))
    xinv = xinv + _bmm(xh, block_diag(resid.astype(BF16)))
    v = stack_heads(jnp.concatenate(side[0:2], axis=-1), A_DV)
    q = stack_heads(jnp.concatenate(side[2:4], axis=-1), A_DK)
    q = q * lax.rsqrt(jnp.sum(q * q, axis=-1, keepdims=True) + EPS) * (A_DK ** -0.5)
    zs = stack_heads(side[4], A_DV)
    rhs = jnp.concatenate([v * bcol, k * (bcol * jnp.exp(gcol))], axis=-1).astype(BF16)
    rhs_rows = jnp.concatenate([rhs[c * nh:(c + 1) * nh].reshape(1, wide, 2 * A_DV) for c in range(nct)], axis=0)
    xb16 = xinv.astype(BF16)
    head_of_col = (col_w // CHUNK)[None]
    sol = stack(lambda h: _bmm(jnp.where(head_of_col == h, xb16, jnp.zeros_like(xb16)), rhs_rows))
    u = sol[:, :, :A_DV]
    wq = jnp.concatenate([sol[:, :, A_DV:], q], axis=1).astype(BF16)
    kend = (k * jnp.exp(glast - gcol)).astype(BF16)
    qk = jnp.einsum('bid,bjd->bij', q.astype(BF16), kend, preferred_element_type=F32).astype(BF16)
    kend_t = jnp.swapaxes(kend, 1, 2)
    dec = jnp.exp(glast)
    s = state_ref[...]
    outs = []
    for c in range(nct):
        sl = slice(c * nh, (c + 1) * nh)
        r = _bmm(wq[sl], s.astype(BF16))
        delta = (u[sl] - r[:, :CHUNK]).astype(BF16)
        outs.append(dec[sl] * r[:, CHUNK:] + _bmm(qk[sl], delta))
        s = dec[sl] * s + _bmm(kend_t[sl], delta)
    state_ref[...] = s
    o = jnp.concatenate(outs, axis=0)
    o = o * lax.rsqrt(jnp.mean(o * o, axis=-1, keepdims=True) + EPS) * norm_ref[...]
    o = (o * _silu(zs)).astype(y_ref.dtype)
    for c in range(nct):
        for h in range(nh):
            y_ref[0, c * CHUNK:(c + 1) * CHUNK, h * A_DV:(h + 1) * A_DV] = o[c * nh + h]


def _mixer_a(x, w, conv, alog, dtb, norm, expand, ts):
    bsz, s, d = x.shape
    wcols = w.shape[1]
    full = lambda shape: pl.BlockSpec(shape, lambda b, j: (0,) * len(shape))
    return pl.pallas_call(
        functools.partial(_mixer_a_kernel, ts=ts),
        grid=(bsz, s // ts),
        in_specs=[pl.BlockSpec((1, ts, d), lambda b, j: (b, j, 0)),
                  full((d, wcols)), full((CONV_K, A_QKV_W)), full((1, LANES)), full((1, LANES)),
                  full((1, A_DV)), full(expand.shape)],
        out_specs=pl.BlockSpec((1, ts, A_V_W), lambda b, j: (b, j, 0)),
        out_shape=jax.ShapeDtypeStruct((bsz, s, A_V_W), BF16),
        scratch_shapes=[pltpu.VMEM((8 + ts, A_QKV_W), F32), pltpu.VMEM((A_HEADS, A_DK, A_DV), F32)],
        compiler_params=pltpu.CompilerParams(dimension_semantics=("parallel", "arbitrary"),
                                             vmem_limit_bytes=VMEM_LIMIT),
        name="mixer_a",
    )(x, w, conv, alog, dtb, norm, expand)


def _mixer_b_kernel(x_ref, w_ref, cos_ref, sin_ref, intra_ref, read_ref, write_ref, cdec_ref, norm_ref,
                    y_ref, state_ref, *, ts):
    nct = ts // CHUNK

    @pl.when(pl.program_id(1) == 0)
    def _():
        state_ref[...] = jnp.zeros_like(state_ref)

    xb = x_ref[0].astype(BF16)
    proj = _dot(xb, w_ref[...])
    cos = cos_ref[...]
    sin = sin_ref[...]
    nh = B_HEADS

    def stack(per_head):
        parts = [per_head(h) for h in range(nh)]
        return jnp.concatenate([parts[h][c:c + 1] for c in range(nct) for h in range(nh)], axis=0)

    def rope(t):
        return t * cos + pltpu.roll(t, B_DK // 2, 1) * sin

    q = stack(lambda h: rope(proj[:, h * B_DK:(h + 1) * B_DK]).reshape(nct, CHUNK, B_DK)).astype(BF16)
    k = stack(lambda h: (rope(proj[:, B_W + h * B_DK:B_W + (h + 1) * B_DK]) * (B_DK ** -0.5))
              .reshape(nct, CHUNK, B_DK))
    v = stack(lambda h: proj[:, 2 * B_W + h * B_DV:2 * B_W + (h + 1) * B_DV].reshape(nct, CHUNK, B_DV)).astype(BF16)
    gate = stack(lambda h: proj[:, 3 * B_W + h * B_DV:3 * B_W + (h + 1) * B_DV].reshape(nct, CHUNK, B_DV))
    tile_heads = lambda ref: jnp.concatenate([ref[...]] * nct, axis=0)
    scores = jnp.einsum('bid,bjd->bij', q, k.astype(BF16), preferred_element_type=F32) * tile_heads(intra_ref)
    o = _bmm(scores.astype(BF16), v)
    kw_t = jnp.swapaxes((k * tile_heads(write_ref)).astype(BF16), 1, 2)
    kv = _bmm(kw_t, v)
    s = state_ref[...]
    cdec = cdec_ref[...]
    prev = []
    for c in range(nct):
        prev.append(s.astype(BF16))
        s = cdec * s + kv[c * nh:(c + 1) * nh]
    state_ref[...] = s
    o = o + _bmm(q, jnp.concatenate(prev, axis=0)) * tile_heads(read_ref)
    mu = jnp.mean(o, axis=-1, keepdims=True)
    oc = o - mu
    var = jnp.mean(oc * oc, axis=-1, keepdims=True)
    norm = jnp.concatenate([norm_ref[:, h * B_DV:(h + 1) * B_DV][None] for h in range(nh)] * nct, axis=0)
    o = (oc * lax.rsqrt(var + EPS) * norm * _silu(gate)).astype(y_ref.dtype)
    for c in range(nct):
        for h in range(nh):
            y_ref[0, c * CHUNK:(c + 1) * CHUNK, h * B_DV:(h + 1) * B_DV] = o[c * nh + h]


def _retention_tables():
    log_gamma = jnp.log1p(-jnp.exp2(-5.0 - jnp.arange(B_HEADS, dtype=F32)))
    idx = jnp.arange(CHUNK, dtype=F32)
    intra = jnp.exp(log_gamma[:, None, None] * jnp.abs(idx[:, None] - idx[None, :]))
    read = jnp.broadcast_to(jnp.exp(log_gamma[:, None] * (idx + 1.0))[:, :, None], (B_HEADS, CHUNK, B_DV))
    write = jnp.broadcast_to(jnp.exp(log_gamma[:, None] * (CHUNK - 1.0 - idx))[:, :, None], (B_HEADS, CHUNK, B_DK))
    cdec = jnp.broadcast_to(jnp.exp(log_gamma * CHUNK)[:, None, None], (B_HEADS, 1, B_DV))
    return intra, read, write, cdec


def _rope_tables(s):
    pos = jnp.arange(s, dtype=F32)
    inv_freq = ROPE_BASE ** (-jnp.arange(0, B_DK, 2, dtype=F32) / B_DK)
    ang = pos[:, None] * inv_freq[None, :]
    cos, sin = jnp.cos(ang), jnp.sin(ang)
    return jnp.concatenate([cos, cos], axis=-1), jnp.concatenate([-sin, sin], axis=-1)


def _mixer_b(x, w, cos2, sin2, tables, norm, ts):
    bsz, s, d = x.shape
    intra, read, write, cdec = tables
    full = lambda shape: pl.BlockSpec(shape, lambda b, j: (0,) * len(shape))
    return pl.pallas_call(
        functools.partial(_mixer_b_kernel, ts=ts),
        grid=(bsz, s // ts),
        in_specs=[pl.BlockSpec((1, ts, d), lambda b, j: (b, j, 0)),
                  full((d, 4 * B_W)),
                  pl.BlockSpec((ts, B_DK), lambda b, j: (j, 0)), pl.BlockSpec((ts, B_DK), lambda b, j: (j, 0)),
                  full(intra.shape), full(read.shape), full(write.shape), full(cdec.shape), full((1, B_W))],
        out_specs=pl.BlockSpec((1, ts, B_W), lambda b, j: (b, j, 0)),
        out_shape=jax.ShapeDtypeStruct((bsz, s, B_W), BF16),
        scratch_shapes=[pltpu.VMEM((B_HEADS, B_DK, B_DV), F32)],
        compiler_params=pltpu.CompilerParams(dimension_semantics=("parallel", "arbitrary"),
                                             vmem_limit_bytes=VMEM_LIMIT),
        name="mixer_b",
    )(x, w, cos2, sin2, intra, read, write, cdec, norm)


def _mixer_c_kernel(x_ref, w_ref, conv_ref, convb_ref, dtb_ref, alog_ref, dskip_ref, norm_ref, expand_ref,
                    y_ref, buf_ref, state_ref, *, ts):
    nct = ts // CHUNK

    @pl.when(pl.program_id(1) == 0)
    def _():
        buf_ref[0:8, :] = jnp.zeros((8, C_XBC_W), F32)
        state_ref[...] = jnp.zeros_like(state_ref)

    xb = x_ref[0].astype(BF16)
    proj = _dot(xb, w_ref[...])
    z = proj[:, :C_INNER]
    xbc = _silu(_causal_conv(buf_ref, proj[:, C_INNER:C_INNER + C_XBC_W], conv_ref, ts, 0, C_XBC_W)
                + convb_ref[...])
    xs = xbc[:, :C_INNER]
    bm = xbc[:, C_INNER:C_INNER + C_GROUPS * C_STATE]
    cm = xbc[:, C_INNER + C_GROUPS * C_STATE:]
    dt = _softplus(proj[:, C_INNER + C_XBC_W:] + dtb_ref[...])
    lstep = dt * (-jnp.exp(alog_ref[...]))
    lc = _chunk_cumsum(lstep)
    expand = expand_ref[...]
    lce = _dot_sel_rhs(lc, expand)
    dte = _dot(dt.astype(BF16), expand)
    ng = C_GROUPS

    def stack(per_group):
        parts = [per_group(g) for g in range(ng)]
        return jnp.concatenate([parts[g][c:c + 1] for c in range(nct) for g in range(ng)], axis=0)

    def group_lanes(a, width=C_GROUP_W):
        return stack(lambda g: a[:, g * width:(g + 1) * width].reshape(nct, CHUNK, width))

    def group_rows(ref):
        return jnp.concatenate([ref[:, g * C_GROUP_W:(g + 1) * C_GROUP_W][None] for g in range(ng)] * nct, axis=0)

    lg = group_lanes(lce)
    dg = group_lanes(dte)
    xg = group_lanes(xs)
    zg = group_lanes(z)
    bmg = group_lanes(bm, C_STATE).astype(BF16)
    cmg = group_lanes(cm, C_STATE).astype(BF16)
    heads_per_group = C_HEADS // ng

    def row_form(a):
        at = jnp.swapaxes(a.reshape(nct, CHUNK, LANES), 1, 2)
        return stack(lambda g: jnp.concatenate(
            [at[:, g * heads_per_group + h:g * heads_per_group + h + 1, :] for h in range(heads_per_group)], axis=-1))

    lc_row = row_form(lc)
    dt_row = row_form(dt)
    bm_t = jnp.concatenate([bmg] * (C_GROUP_W // CHUNK), axis=1)
    cbt = jnp.einsum('bin,bmn->bim', cmg, bm_t, preferred_element_type=F32)
    wm = (cbt * jnp.exp(-jnp.abs(lg - lc_row)) * dt_row).astype(BF16)
    lane = lax.broadcasted_iota(jnp.int32, (1, 1, LANES), 2)
    left = lane < C_HEAD_DIM
    pair_out = []
    for p in range(C_GROUP_W // LANES):
        xp = xg[:, :, p * LANES:(p + 1) * LANES]
        bd = jnp.concatenate([jnp.where(left, xp, 0.0), jnp.where(left, 0.0, xp)], axis=1)
        pair_out.append(_bmm(wm[:, :, p * LANES:(p + 1) * LANES], bd.astype(BF16)))
    y = jnp.concatenate(pair_out, axis=-1)
    llast = lg[:, CHUNK - 1:CHUNK, :]
    xw = (xg * (jnp.exp(llast - lg) * dg)).astype(BF16)
    kv = _bmm(jnp.swapaxes(bmg, 1, 2), xw)
    sdec = jnp.exp(llast)
    st = state_ref[...]
    prev = []
    for ch in range(nct):
        sl = slice(ch * ng, (ch + 1) * ng)
        prev.append(st.astype(BF16))
        st = sdec[sl] * st + kv[sl]
    state_ref[...] = st
    y = y + _bmm(cmg, jnp.concatenate(prev, axis=0)) * jnp.exp(lg)
    y = (y + group_rows(dskip_ref) * xg) * _silu(zg)
    y = (y * lax.rsqrt(jnp.mean(y * y, axis=-1, keepdims=True) + EPS) * group_rows(norm_ref)).astype(y_ref.dtype)
    for ch in range(nct):
        for g in range(ng):
            y_ref[0, ch * CHUNK:(ch + 1) * CHUNK, g * C_GROUP_W:(g + 1) * C_GROUP_W] = y[ch * ng + g]


def _mixer_c(x, w, conv, convb, dtb, alog, dskip, norm, expand, ts):
    bsz, s, d = x.shape
    wcols = w.shape[1]
    full = lambda shape: pl.BlockSpec(shape, lambda b, j: (0,) * len(shape))
    return pl.pallas_call(
        functools.partial(_mixer_c_kernel, ts=ts),
        grid=(bsz, s // ts),
        in_specs=[pl.BlockSpec((1, ts, d), lambda b, j: (b, j, 0)),
                  full((d, wcols)), full((CONV_K, C_XBC_W)), full((1, C_XBC_W)), full((1, LANES)),
                  full((1, LANES)), full((1, C_INNER)), full((1, C_INNER)), full((LANES, C_INNER))],
        out_specs=pl.BlockSpec((1, ts, C_INNER), lambda b, j: (b, j, 0)),
        out_shape=jax.ShapeDtypeStruct((bsz, s, C_INNER), BF16),
        scratch_shapes=[pltpu.VMEM((8 + ts, C_XBC_W), F32), pltpu.VMEM((C_GROUPS, C_STATE, C_GROUP_W), F32)],
        compiler_params=pltpu.CompilerParams(dimension_semantics=("parallel", "arbitrary"),
                                             vmem_limit_bytes=VMEM_LIMIT),
        name="mixer_c",
    )(x, w, conv, convb, dtb, alog, dskip, norm, expand)


def _layernorm(h, g, b):
    mu = jnp.mean(h, axis=-1, keepdims=True)
    hc = h - mu
    var = jnp.mean(hc * hc, axis=-1, keepdims=True)
    return hc * lax.rsqrt(var + EPS) * g + b


def _merge_kernel(x_ref, ya_ref, yb_ref, yc_ref, wg_ref, bg_ref, wa_ref, wb_ref, wc_ref, wo_ref,
                  lng_ref, lnb_ref, wrh_ref, wrl_ref, br_ref, x1_ref, x1b_ref, ids_ref, wts_ref, cnt_ref,
                  run_ref):
    @pl.when(pl.program_id(0) == 0)
    def _():
        run_ref[...] = jnp.zeros_like(run_ref)

    x = x_ref[...]
    xb = x.astype(BF16)
    merged = None
    for i, (y_ref, w_ref) in enumerate(((ya_ref, wa_ref), (yb_ref, wb_ref), (yc_ref, wc_ref))):
        gate = _sigmoid(_dot(xb, wg_ref[:, i * D_MODEL:(i + 1) * D_MODEL]) + bg_ref[i:i + 1, :])
        term = gate * _dot(y_ref[...], w_ref[...])
        merged = term if merged is None else merged + term
    mix = _dot(merged.astype(BF16), wo_ref[...])
    x1 = _layernorm(DEEPNORM_ALPHA * x + mix, lng_ref[...], lnb_ref[...])
    x1_ref[...] = x1
    x1b_ref[...] = x1.astype(BF16)

    xh, xl = _split2(x1)
    logits = _dot(xh, wrh_ref[...]) + _dot(xl, wrh_ref[...]) + _dot(xh, wrl_ref[...]) + br_ref[...]
    tm = logits.shape[0]
    lane = lax.broadcasted_iota(jnp.int32, logits.shape, 1)
    neg = jnp.float32(-jnp.inf)
    gmask = lane < N_GROUPS
    gl = jnp.where(gmask, logits, neg)
    gmax = jnp.max(gl, axis=-1, keepdims=True)
    gidx = jnp.min(jnp.where(gmask & (gl == gmax), lane, LANES), axis=-1, keepdims=True)
    gp = 1.0 / jnp.sum(jnp.where(gmask, jnp.exp(gl - gmax), 0.0), axis=-1, keepdims=True)
    emask = (lane >= N_GROUPS) & (lane < N_GROUPS + N_EXPERTS) & ((lane - N_GROUPS) // EXPERTS_PER_GROUP == gidx)
    el = jnp.where(emask, logits, neg)
    v1 = jnp.max(el, axis=-1, keepdims=True)
    i1 = jnp.min(jnp.where(emask & (el == v1), lane, LANES), axis=-1, keepdims=True)
    el2 = jnp.where(lane == i1, neg, el)
    v2 = jnp.max(el2, axis=-1, keepdims=True)
    i2 = jnp.min(jnp.where(emask & (lane != i1) & (el2 == v2), lane, LANES), axis=-1, keepdims=True)
    e21 = jnp.exp(v2 - v1)
    w1 = gp / (1.0 + e21)
    w2 = gp * e21 / (1.0 + e21)

    hot1 = lane == i1
    hot2 = lane == i2
    hot = jnp.where(hot1 | hot2, 1.0, 0.0)
    r = lax.broadcasted_iota(jnp.int32, (tm, tm), 0)
    c = lax.broadcasted_iota(jnp.int32, (tm, tm), 1)
    before = _dot(jnp.where(c < r, 1.0, 0.0).astype(BF16), hot.astype(BF16)) + run_ref[0:1, :]
    rank1 = jnp.sum(jnp.where(hot1, before, 0.0), axis=-1, keepdims=True)
    rank2 = jnp.sum(jnp.where(hot2, before, 0.0), axis=-1, keepdims=True)
    run_ref[0:1, :] = run_ref[0:1, :] + jnp.sum(hot, axis=0, keepdims=True)
    cnt_ref[...] = jnp.broadcast_to(run_ref[0:1, :], cnt_ref.shape)

    ids = jnp.where(lane == 0, (i1 - N_GROUPS).astype(F32), jnp.where(lane == 1, (i2 - N_GROUPS).astype(F32),
                    jnp.where(lane == 2, rank1, jnp.where(lane == 3, rank2, 0.0))))
    ids_ref[...] = jnp.transpose(ids)[0:8, :].astype(jnp.int32)
    wts_ref[...] = jnp.where(lane == 0, w1, jnp.where(lane == 1, w2, 0.0))


def _merge(x2d, ya, yb, yc, wg, bg, wa, wb, wc, wo, lng, lnb, wrh, wrl, br, tm):
    t, d = x2d.shape
    full = lambda shape: pl.BlockSpec(shape, lambda i: (0,) * len(shape))
    row = lambda w: pl.BlockSpec((tm, w), lambda i: (i, 0))
    return pl.pallas_call(
        _merge_kernel,
        grid=(t // tm,),
        in_specs=[row(d), row(A_V_W), row(B_W), row(C_INNER),
                  full(wg.shape), full(bg.shape), full(wa.shape), full(wb.shape), full(wc.shape), full(wo.shape),
                  full((1, d)), full((1, d)), full(wrh.shape), full(wrl.shape), full((1, LANES))],
        out_specs=[row(d), row(d), pl.BlockSpec((8, tm), lambda i: (0, i)), row(LANES), full((8, LANES))],
        out_shape=[jax.ShapeDtypeStruct((t, d), F32), jax.ShapeDtypeStruct((t, d), BF16),
                   jax.ShapeDtypeStruct((8, t), jnp.int32), jax.ShapeDtypeStruct((t, LANES), F32),
                   jax.ShapeDtypeStruct((8, LANES), F32)],
        scratch_shapes=[pltpu.VMEM((8, LANES), F32)],
        compiler_params=pltpu.CompilerParams(dimension_semantics=("arbitrary",), vmem_limit_bytes=VMEM_LIMIT),
        name="merge_ln_router",
    )(x2d, ya, yb, yc, wg, bg, wa, wb, wc, wo, lng, lnb, wrh, wrl, br)


def _experts_kernel(te_ref, nv_ref, xs_ref, wg_ref, wu_ref, wd_ref, *rest, first_tile):
    ys_ref, wgb_ref, wub_ref, wdb_ref = rest[-4:]
    i = pl.program_id(0)
    tile = i + first_tile

    @pl.when((i == 0) | (te_ref[tile] != te_ref[jnp.maximum(tile - 1, 0)]))
    def _():
        wgb_ref[...] = wg_ref[0, 0].astype(BF16)
        wub_ref[...] = wu_ref[0, 0].astype(BF16)
        wdb_ref[...] = wd_ref[0, 0].astype(BF16)

    @pl.when(tile < nv_ref[0])
    def _():
        xs = xs_ref[...]
        h = _silu(_dot(xs, wgb_ref[...])) * _dot(xs, wub_ref[...])
        ys_ref[...] = _dot(h.astype(BF16), wdb_ref[...]).astype(ys_ref.dtype)

    @pl.when(tile >= nv_ref[0])
    def _():
        ys_ref[...] = jnp.zeros_like(ys_ref)


def _experts(tile_expert, n_valid, xs, wg, wu, wd, layer, tm, first_tile, total_rows, ys_prev):
    rows, d = xs.shape
    wspec = lambda shape: pl.BlockSpec((1, 1) + shape, lambda i, te, nv: (layer, te[i + first_tile], 0, 0))
    in_specs = [pl.BlockSpec((tm, d), lambda i, te, nv: (i, 0)),
                wspec((d, D_EXPERT)), wspec((d, D_EXPERT)), wspec((D_EXPERT, d))]
    args = [tile_expert, n_valid, xs, wg, wu, wd]
    aliases = {}
    if ys_prev is not None:
        in_specs.append(pl.BlockSpec(memory_space=pl.ANY))
        aliases = {len(args): 0}
        args.append(ys_prev)
    grid_spec = pltpu.PrefetchScalarGridSpec(
        num_scalar_prefetch=2,
        grid=(rows // tm,),
        in_specs=in_specs,
        out_specs=pl.BlockSpec((tm, d), lambda i, te, nv: (i + first_tile, 0)),
        scratch_shapes=[pltpu.VMEM((d, D_EXPERT), BF16), pltpu.VMEM((d, D_EXPERT), BF16),
                        pltpu.VMEM((D_EXPERT, d), BF16)],
    )
    return pl.pallas_call(
        functools.partial(_experts_kernel, first_tile=first_tile),
        grid_spec=grid_spec,
        out_shape=jax.ShapeDtypeStruct((total_rows, d), BF16),
        input_output_aliases=aliases,
        compiler_params=pltpu.CompilerParams(dimension_semantics=("arbitrary",), vmem_limit_bytes=VMEM_LIMIT),
        name="grouped_experts",
    )(*args)


def _combine_kernel(x_ref, g_ref, wts_ref, lng_ref, lnb_ref, *rest):
    o_ref = rest[-1]
    wts = wts_ref[...]
    ffn = wts[:, 0:1] * g_ref[0].astype(F32) + wts[:, 1:2] * g_ref[1].astype(F32)
    o_ref[...] = _layernorm(DEEPNORM_ALPHA * x_ref[...] + ffn, lng_ref[...], lnb_ref[...])


def _combine(x1, g, wts, lng, lnb, tm, first_tile, out_prev):
    t, d = x1.shape
    row = lambda w: pl.BlockSpec((tm, w), lambda i: (i + first_tile, 0))
    vec = pl.BlockSpec((1, d), lambda i: (0, 0))
    in_specs = [row(d), pl.BlockSpec((2, tm, d), lambda i: (0, i, 0)), row(LANES), vec, vec]
    args = [x1, g, wts, lng, lnb]
    aliases = {}
    if out_prev is not None:
        in_specs.append(pl.BlockSpec(memory_space=pl.ANY))
        aliases = {len(args): 0}
        args.append(out_prev)
    return pl.pallas_call(
        _combine_kernel,
        grid=(g.shape[1] // tm,),
        in_specs=in_specs,
        out_specs=row(d),
        out_shape=jax.ShapeDtypeStruct((t, d), F32),
        input_output_aliases=aliases,
        compiler_params=pltpu.CompilerParams(dimension_semantics=("parallel",), vmem_limit_bytes=VMEM_LIMIT),
        name="combine_ln",
    )(*args)


def _pad_lanes(v, width=LANES):
    v = v.reshape(1, -1).astype(F32)
    return jnp.pad(v, ((0, 0), (0, width - v.shape[1])))


def _rows(a, idx):
    return a.at[idx].get(mode="promise_in_bounds")


def _dispatch(ids, counts, t, tm):
    n_pairs = 2 * t
    n_tiles = n_pairs // tm + N_EXPERTS
    n_rows = n_tiles * tm
    e1, e2, rank1, rank2 = ids[0], ids[1], ids[2], ids[3]
    padded = ((counts + tm - 1) // tm) * tm
    pend = jnp.cumsum(padded)
    pstart = pend - padded
    pos = jnp.concatenate([_rows(pstart, e1) + rank1, _rows(pstart, e2) + rank2])
    count_le = lambda table, v: jnp.sum((table[None, :] <= v[:, None]).astype(jnp.int32), axis=1)
    span = 2 * n_pairs
    tok2 = 2 * jnp.arange(t, dtype=jnp.int32)
    slot = jnp.arange(n_rows - n_pairs, dtype=jnp.int32)
    slot_e = count_le(jnp.cumsum(padded - counts), slot)
    keys = jnp.concatenate([e1 * span + tok2, e2 * span + tok2 + 1, slot_e * span + n_pairs + slot % n_pairs])
    low = lax.sort(keys) % span
    row_tok = jnp.where(low < n_pairs, low // 2, jnp.arange(n_rows, dtype=jnp.int32) % t)
    n_valid = (pend[-1] // tm).astype(jnp.int32).reshape(1)
    tile_start = jnp.arange(n_tiles, dtype=jnp.int32) * tm
    tile_e = jnp.minimum(count_le(pend, tile_start), count_le(pend, pend[-1:] - 1))
    return row_tok, pos, tile_e, n_valid


def _layer(x, p, consts, seq_tile, token_tile, expert_tile):
    bsz, s, d = x.shape
    t = bsz * s
    ya = _mixer_a(x, p["w_a"], p["conv_a"], p["a_log_a"], p["dt_bias_a"], p["norm_a"], consts["expand_a"],
                  seq_tile)
    yb = _mixer_b(x, p["w_b"], consts["cos2"], consts["sin2"], consts["ret"], p["norm_b"], seq_tile)
    yc = _mixer_c(x, p["w_c"], p["conv_c"], p["conv_bias_c"], p["dt_bias_c"], p["a_log_c"], p["d_skip_c"],
                  p["norm_c"], consts["expand"], seq_tile)
    x1, x1b, ids, wts, cnt = _merge(x.reshape(t, d), ya.reshape(t, -1), yb.reshape(t, -1), yc.reshape(t, -1),
                                    p["w_gate_in"], p["b_gate"], p["w_branch_a"], p["w_branch_b"],
                                    p["w_branch_c"], p["w_out"], p["ln1_g"], p["ln1_b"], p["w_router_hi"],
                                    p["w_router_lo"], p["b_router"], token_tile)
    counts = cnt[0, N_GROUPS:N_GROUPS + N_EXPERTS].astype(jnp.int32)
    row_tok, pos, tile_e, n_valid = _dispatch(ids, counts, t, expert_tile)
    n_rows = row_tok.shape[0]
    rows_c = n_rows // DISPATCH_CHUNKS
    ys = None
    for k in range(DISPATCH_CHUNKS):
        xs = _rows(x1b, row_tok[k * rows_c:(k + 1) * rows_c])
        ys = _experts(tile_e, n_valid, xs, p["w_gate_e"], p["w_up_e"], p["w_down_e"], p["layer"], expert_tile,
                      k * (rows_c // expert_tile), n_rows, ys)
    pos = pos.reshape(2, t)
    t_c = t // DISPATCH_CHUNKS
    x2 = None
    for k in range(DISPATCH_CHUNKS):
        g = _rows(ys, pos[:, k * t_c:(k + 1) * t_c].reshape(-1)).reshape(2, t_c, d)
        x2 = _combine(x1, g, wts, p["ln2_g"], p["ln2_b"], token_tile, k * (t_c // token_tile), x2)
    return x2.reshape(bsz, s, d)


def _layer_params(l, w_in, conv_a, a_log_a, dt_bias_a, norm_a, norm_b, conv_c, conv_bias_c, dt_bias_c,
                  a_log_c, d_skip_c, norm_c, b_gate, w_branch_a, w_branch_b, w_branch_c, w_out,
                  ln1_g, ln1_b, w_router_group, b_router_group, w_router_expert, b_router_expert,
                  w_gate_e, w_up_e, w_down_e, ln2_g, ln2_b):
    o = IN_OFFS
    w = w_in[l]
    d = w.shape[0]
    zpad = lambda n: jnp.zeros((d, n), F32)
    w_a = jnp.concatenate([w[:, o[0]:o[2]], w[:, o[2]:o[3]], zpad(LANES - A_HEADS),
                           w[:, o[3]:o[4]], zpad(LANES - A_HEADS)], axis=1).astype(BF16)
    w_b = w[:, o[4]:o[8]].astype(BF16)
    w_c = jnp.concatenate([w[:, o[8]:o[10]], w[:, o[10]:o[11]], zpad(LANES - C_HEADS)], axis=1).astype(BF16)
    w_router = jnp.concatenate([w_router_group[l], w_router_expert[l],
                                zpad(LANES - N_GROUPS - N_EXPERTS)], axis=1)
    b_router = _pad_lanes(jnp.concatenate([b_router_group[l], b_router_expert[l]]))
    return {
        "w_a": w_a, "w_b": w_b, "w_c": w_c, "w_gate_in": w[:, o[11]:o[12]].astype(BF16),
        "conv_a": conv_a[l], "a_log_a": _pad_lanes(a_log_a[l]), "dt_bias_a": _pad_lanes(dt_bias_a[l]),
        "norm_a": norm_a[l].reshape(1, -1), "norm_b": norm_b[l].reshape(1, -1),
        "conv_c": conv_c[l], "conv_bias_c": conv_bias_c[l].reshape(1, -1),
        "dt_bias_c": _pad_lanes(dt_bias_c[l]), "a_log_c": _pad_lanes(a_log_c[l]),
        "d_skip_c": jnp.repeat(d_skip_c[l], C_HEAD_DIM).reshape(1, -1), "norm_c": norm_c[l].reshape(1, -1),
        "b_gate": b_gate[l],
        "w_branch_a": w_branch_a[l].astype(BF16), "w_branch_b": w_branch_b[l].astype(BF16),
        "w_branch_c": w_branch_c[l].astype(BF16), "w_out": w_out[l].astype(BF16),
        "ln1_g": ln1_g[l].reshape(1, -1), "ln1_b": ln1_b[l].reshape(1, -1),
        "w_router_hi": w_router.astype(BF16),
        "w_router_lo": (w_router - w_router.astype(BF16).astype(F32)).astype(BF16), "b_router": b_router,
        "w_gate_e": w_gate_e, "w_up_e": w_up_e, "w_down_e": w_down_e, "layer": l,
        "ln2_g": ln2_g[l].reshape(1, -1), "ln2_b": ln2_b[l].reshape(1, -1),
    }


def _head_expand(heads, width):
    head_of_lane = jnp.arange(heads * width) // width
    return (jnp.arange(LANES)[:, None] == head_of_lane[None, :]).astype(BF16)


def _forward(x, params, seq_tile=SEQ_TILE, token_tile=TOKEN_TILE, expert_tile=EXPERT_TILE):
    s = x.shape[1]
    cos2, sin2 = _rope_tables(s)
    consts = {"cos2": cos2, "sin2": sin2, "ret": _retention_tables(),
              "expand": _head_expand(C_HEADS, C_HEAD_DIM), "expand_a": _head_expand(A_HEADS, CHUNK)}
    for l in range(DEPTH):
        x = _layer(x, _layer_params(l, *params), consts, seq_tile, token_tile, expert_tile)
    return x


def kernel(x, w_in, conv_a, a_log_a, dt_bias_a, norm_a, norm_b, conv_c, conv_bias_c, dt_bias_c, a_log_c, d_skip_c, norm_c, b_gate, w_branch_a, w_branch_b, w_branch_c, w_out, ln1_g, ln1_b, w_router_group, b_router_group, w_router_expert, b_router_expert, w_gate_e, w_up_e, w_down_e, ln2_g, ln2_b):
    params = (w_in, conv_a, a_log_a, dt_bias_a, norm_a, norm_b, conv_c, conv_bias_c, dt_bias_c, a_log_c,
              d_skip_c, norm_c, b_gate, w_branch_a, w_branch_b, w_branch_c, w_out, ln1_g, ln1_b,
              w_router_group, b_router_group, w_router_expert, b_router_expert, w_gate_e, w_up_e, w_down_e,
              ln2_g, ln2_b)
    return _forward(x, params)
```

```python
import functools
import math

import jax
import jax.numpy as jnp
from jax import lax
from jax.experimental import pallas as pl
from jax.experimental.pallas import tpu as pltpu

F32 = jnp.float32
BF16 = jnp.bfloat16

D_MODEL = 1024
DEPTH = 2
CHUNK = 64
CONV_K = 4
EPS = 1e-6
LANES = 128
A_HEADS = 4
A_DK = 128
A_DV = 128
A_QKV_W = 3 * A_HEADS * A_DK
A_V_W = A_HEADS * A_DV
B_HEADS = 4
B_DK = 128
B_DV = 128
B_W = B_HEADS * B_DK
ROPE_BASE = 10000.0
C_INNER = 1024
C_HEAD_DIM = 64
C_HEADS = 16
C_GROUPS = 2
C_STATE = 128
C_XBC_W = C_INNER + 2 * C_GROUPS * C_STATE
C_GROUP_W = C_INNER // C_GROUPS
N_BRANCH = 3
N_GROUPS = 4
EXPERTS_PER_GROUP = 8
N_EXPERTS = 32
D_EXPERT = 512
DEEPNORM_ALPHA = (2 * DEPTH) ** 0.25

IN_SIZES = (A_QKV_W, A_V_W, A_HEADS, A_HEADS, B_W, B_W, B_W, B_W, C_INNER, C_XBC_W, C_HEADS,
            N_BRANCH * D_MODEL)
IN_OFFS = tuple(int(sum(IN_SIZES[:i])) for i in range(len(IN_SIZES) + 1))

SEQ_TILE = 512
A_SEQS = 4
TOKEN_TILE = 512
EXPERT_TILE = 512
DISPATCH_CHUNKS = 4
VMEM_LIMIT = 56 * 1024 * 1024


def _dot(a, b):
    return jnp.dot(a, b, preferred_element_type=F32)


def _split2(a):
    hi = a.astype(BF16)
    return hi, (a - hi.astype(F32)).astype(BF16)


def _bmm(a, b):
    return jnp.einsum('cij,cjk->cik', a, b, preferred_element_type=F32)


def _dot_sel_rhs(a, sel):
    hi, lo = _split2(a)
    return _dot(hi, sel) + _dot(lo, sel)


def _sigmoid(x):
    return 0.5 * jnp.tanh(0.5 * x) + 0.5


def _silu(x):
    h = 0.5 * x
    return h + h * jnp.tanh(h)


def _softplus(x):
    return jnp.maximum(x, 0.0) + jnp.log(1.0 + jnp.exp(-jnp.abs(x)))


def _chunk_cumsum(a):
    ts, lanes = a.shape
    nct = ts // CHUNK
    r = lax.broadcasted_iota(jnp.int32, (nct, CHUNK, CHUNK), 1)
    c = lax.broadcasted_iota(jnp.int32, (nct, CHUNK, CHUNK), 2)
    tri = jnp.where(c <= r, 1.0, 0.0).astype(BF16)
    hi, lo = _split2(a.reshape(nct, CHUNK, lanes))
    return (_bmm(tri, hi) + _bmm(tri, lo)).reshape(ts, lanes)


def _causal_conv(buf_ref, pre, conv_ref, ts, c0, c1):
    buf_ref[8:8 + ts, c0:c1] = pre
    acc = buf_ref[pl.ds(8 - (CONV_K - 1), ts), c0:c1] * conv_ref[0:1, c0:c1]
    for k in range(1, CONV_K):
        acc = acc + buf_ref[pl.ds(8 - (CONV_K - 1) + k, ts), c0:c1] * conv_ref[k:k + 1, c0:c1]
    buf_ref[8 - (CONV_K - 1):8, c0:c1] = buf_ref[8 + ts - (CONV_K - 1):8 + ts, c0:c1]
    return acc


def _mixer_a_kernel(x_ref, w_ref, conv_ref, alog_ref, dtb_ref, norm_ref, e64_ref, y_ref, buf_ref, state_ref,
                    *, ts, nseq):
    nct_seq = ts // CHUNK
    nct = nseq * nct_seq

    @pl.when(pl.program_id(1) == 0)
    def _():
        buf_ref[:, 0:8, :] = jnp.zeros((nseq, 8, A_QKV_W), F32)
        state_ref[...] = jnp.zeros_like(state_ref)

    xb = x_ref[...].reshape(nseq * ts, x_ref.shape[2]).astype(BF16)
    nh = A_HEADS
    qk_w = nh * A_DK

    def conv_cols(c0, c1):
        pre = _dot(xb, w_ref[:, c0:c1])
        return _silu(jnp.concatenate(
            [_causal_conv(buf_ref.at[si], pre[si * ts:(si + 1) * ts], conv_ref, ts, c0, c1) for si in range(nseq)],
            axis=0))

    small = _dot(xb, w_ref[:, 2048:2048 + 2 * LANES])
    a_raw = small[:, :LANES]
    b_raw = small[:, LANES:]
    kf = conv_cols(qk_w, 2 * qk_w)
    beta = _sigmoid(b_raw)
    g = -jnp.exp(alog_ref[...]) * _softplus(a_raw + dtb_ref[...])
    gc = _chunk_cumsum(g)
    gc3 = gc.reshape(nct, CHUNK, LANES)
    beta3 = beta.reshape(nct, CHUNK, LANES)
    gct3 = jnp.swapaxes(gc3, 1, 2)
    wide = nh * CHUNK

    def stack(per_head):
        parts = [per_head(h) for h in range(nh)]
        return jnp.concatenate([parts[h][c:c + 1] for c in range(nct) for h in range(nh)], axis=0)

    def stack_heads(a, width):
        return stack(lambda h: a[:, h * width:(h + 1) * width].reshape(nct, CHUNK, width))

    k = stack_heads(kf, A_DK)
    gcol = stack(lambda h: gc3[:, :, h:h + 1])
    bcol = stack(lambda h: beta3[:, :, h:h + 1])
    glast = stack(lambda h: gc3[:, CHUNK - 1:CHUNK, h:h + 1])
    k = k * lax.rsqrt(jnp.sum(k * k, axis=-1, keepdims=True) + EPS)

    row_w = lax.broadcasted_iota(jnp.int32, (CHUNK, wide), 0)
    col_w = lax.broadcasted_iota(jnp.int32, (CHUNK, wide), 1)
    strict = (col_w % CHUNK < row_w)[None]
    eye = jnp.where(col_w % CHUNK == row_w, 1.0, 0.0).astype(F32)[None]
    blk_r = lax.broadcasted_iota(jnp.int32, (wide, wide), 0) // CHUNK
    blk_c = lax.broadcasted_iota(jnp.int32, (wide, wide), 1) // CHUNK
    same_head = (blk_r == blk_c)[None]

    def block_diag(m):
        tiled = jnp.concatenate([m] * nh, axis=1)
        return jnp.where(same_head, tiled, jnp.zeros_like(tiled))

    gcol_w = _dot_sel_rhs(gc, e64_ref[...]).reshape(nct, CHUNK, wide)
    beta_w = _dot(beta.astype(BF16), e64_ref[...]).reshape(nct, CHUNK, wide)
    grow_w = jnp.concatenate([gct3[:, h:h + 1, :] for h in range(nh)], axis=-1)
    decay = jnp.where(strict, jnp.exp(jnp.where(strict, gcol_w - grow_w, 0.0)), 0.0)
    kn = jnp.concatenate([k[c * nh:(c + 1) * nh].reshape(1, wide, A_DK) for c in range(nct)], axis=0)
    kn = kn.astype(BF16)
    k_nat = jnp.concatenate([jnp.concatenate([k[c * nh + h][None] for c in range(nct)], axis=0)
                             for h in range(nh)], axis=-1).astype(BF16)
    head_r = lax.broadcasted_iota(jnp.int32, (wide, qk_w), 0) // CHUNK
    head_c = lax.broadcasted_iota(jnp.int32, (wide, qk_w), 1) // A_DK
    k_bd = jnp.where((head_r == head_c)[None], jnp.concatenate([kn] * nh, axis=-1), jnp.zeros((), BF16))
    amat = beta_w * jnp.einsum('bid,bjd->bij', k_nat, k_bd, preferred_element_type=F32) * decay
    half = qk_w // 2
    side_work = [lambda: conv_cols(2 * qk_w, 2 * qk_w + half), lambda: conv_cols(2 * qk_w + half, 3 * qk_w),
                 lambda: conv_cols(0, half), lambda: conv_cols(half, qk_w),
                 lambda: _dot(xb, w_ref[:, A_QKV_W:A_QKV_W + A_V_W])]
    side = []
    xinv = eye - amat
    pw = amat.astype(BF16)
    pw_bd = block_diag(pw)
    for step in range(5):
        pw = _bmm(pw, pw_bd).astype(BF16)
        pw_bd = block_diag(pw)
        side.append(side_work[step]())
        xinv = xinv + _bmm(xinv.astype(BF16), pw_bd)
    ah, al = _split2(amat)
    xh, xl = _split2(xinv)
    xh_bd = block_diag(xh)
    resid = eye - xinv - (_bmm(ah, xh_bd) + _bmm(ah, block_diag(xl)) + _bmm(al, xh_bd))
    xinv = xinv + _bmm(xh, block_diag(resid.astype(BF16)))
    v = stack_heads(jnp.concatenate(side[0:2], axis=-1), A_DV)
    q = stack_heads(jnp.concatenate(side[2:4], axis=-1), A_DK)
    q = q * lax.rsqrt(jnp.sum(q * q, axis=-1, keepdims=True) + EPS) * (A_DK ** -0.5)
    zs = stack_heads(side[4], A_DV)
    rhs = jnp.concatenate([v * bcol, k * (bcol * jnp.exp(gcol))], axis=-1).astype(BF16)
    rhs_rows = jnp.concatenate([rhs[c * nh:(c + 1) * nh].reshape(1, wide, 2 * A_DV) for c in range(nct)], axis=0)
    xb16 = xinv.astype(BF16)
    head_of_col = (col_w // CHUNK)[None]
    sol = stack(lambda h: _bmm(jnp.where(head_of_col == h, xb16, jnp.zeros_like(xb16)), rhs_rows))
    u = sol[:, :, :A_DV]
    wq = jnp.concatenate([sol[:, :, A_DV:], q], axis=1).astype(BF16)
    kend = (k * jnp.exp(glast - gcol)).astype(BF16)
    qk = jnp.einsum('bid,bjd->bij', q.astype(BF16), kend, preferred_element_type=F32).astype(BF16)
    kend_t = jnp.swapaxes(kend, 1, 2)
    dec = jnp.exp(glast)
    s = state_ref[...]

    def step_rows(a, c):
        return jnp.concatenate([a[(si * nct_seq + c) * nh:(si * nct_seq + c + 1) * nh] for si in range(nseq)], axis=0)

    outs = []
    for c in range(nct_seq):
        dec_c = step_rows(dec, c)
        r = _bmm(step_rows(wq, c), s.astype(BF16))
        delta = (step_rows(u, c) - r[:, :CHUNK]).astype(BF16)
        outs.append(dec_c * r[:, CHUNK:] + _bmm(step_rows(qk, c), delta))
        s = dec_c * s + _bmm(step_rows(kend_t, c), delta)
    state_ref[...] = s
    o = jnp.concatenate([outs[c][si * nh:(si + 1) * nh] for si in range(nseq) for c in range(nct_seq)], axis=0)
    o = o * lax.rsqrt(jnp.mean(o * o, axis=-1, keepdims=True) + EPS) * norm_ref[...]
    o = (o * _silu(zs)).astype(y_ref.dtype)
    for si in range(nseq):
        for c in range(nct_seq):
            for h in range(nh):
                y_ref[si, c * CHUNK:(c + 1) * CHUNK, h * A_DV:(h + 1) * A_DV] = o[((si * nct_seq + c) * nh) + h]


def _mixer_a(x, w, conv, alog, dtb, norm, expand, ts, nseq):
    bsz, s, d = x.shape
    wcols = w.shape[1]
    full = lambda shape: pl.BlockSpec(shape, lambda b, j: (0,) * len(shape))
    return pl.pallas_call(
        functools.partial(_mixer_a_kernel, ts=ts, nseq=nseq),
        grid=(bsz // nseq, s // ts),
        in_specs=[pl.BlockSpec((nseq, ts, d), lambda b, j: (b, j, 0)),
                  full((d, wcols)), full((CONV_K, A_QKV_W)), full((1, LANES)), full((1, LANES)),
                  full((1, A_DV)), full(expand.shape)],
        out_specs=pl.BlockSpec((nseq, ts, A_V_W), lambda b, j: (b, j, 0)),
        out_shape=jax.ShapeDtypeStruct((bsz, s, A_V_W), BF16),
        scratch_shapes=[pltpu.VMEM((nseq, 8 + ts, A_QKV_W), F32),
                        pltpu.VMEM((nseq * A_HEADS, A_DK, A_DV), F32)],
        compiler_params=pltpu.CompilerParams(dimension_semantics=("parallel", "arbitrary"),
                                             vmem_limit_bytes=VMEM_LIMIT),
        name="mixer_a",
    )(x, w, conv, alog, dtb, norm, expand)


def _mixer_b_kernel(x_ref, w_ref, cos_ref, sin_ref, intra_ref, read_ref, write_ref, cdec_ref, norm_ref,
                    y_ref, state_ref, *, ts):
    nct = ts // CHUNK

    @pl.when(pl.program_id(1) == 0)
    def _():
        state_ref[...] = jnp.zeros_like(state_ref)

    xb = x_ref[0].astype(BF16)
    proj = _dot(xb, w_ref[...])
    cos = cos_ref[...]
    sin = sin_ref[...]
    nh = B_HEADS

    def stack(per_head):
        parts = [per_head(h) for h in range(nh)]
        return jnp.concatenate([parts[h][c:c + 1] for c in range(nct) for h in range(nh)], axis=0)

    def rope(t):
        return t * cos + pltpu.roll(t, B_DK // 2, 1) * sin

    q = stack(lambda h: rope(proj[:, h * B_DK:(h + 1) * B_DK]).reshape(nct, CHUNK, B_DK)).astype(BF16)
    k = stack(lambda h: (rope(proj[:, B_W + h * B_DK:B_W + (h + 1) * B_DK]) * (B_DK ** -0.5))
              .reshape(nct, CHUNK, B_DK))
    v = stack(lambda h: proj[:, 2 * B_W + h * B_DV:2 * B_W + (h + 1) * B_DV].reshape(nct, CHUNK, B_DV)).astype(BF16)
    gate = stack(lambda h: proj[:, 3 * B_W + h * B_DV:3 * B_W + (h + 1) * B_DV].reshape(nct, CHUNK, B_DV))
    tile_heads = lambda ref: jnp.concatenate([ref[...]] * nct, axis=0)
    scores = jnp.einsum('bid,bjd->bij', q, k.astype(BF16), preferred_element_type=F32) * tile_heads(intra_ref)
    o = _bmm(scores.astype(BF16), v)
    kw_t = jnp.swapaxes((k * tile_heads(write_ref)).astype(BF16), 1, 2)
    kv = _bmm(kw_t, v)
    s = state_ref[...]
    cdec = cdec_ref[...]
    prev = []
    for c in range(nct):
        prev.append(s.astype(BF16))
        s = cdec * s + kv[c * nh:(c + 1) * nh]
    state_ref[...] = s
    o = o + _bmm(q, jnp.concatenate(prev, axis=0)) * tile_heads(read_ref)
    mu = jnp.mean(o, axis=-1, keepdims=True)
    oc = o - mu
    var = jnp.mean(oc * oc, axis=-1, keepdims=True)
    norm = jnp.concatenate([norm_ref[:, h * B_DV:(h + 1) * B_DV][None] for h in range(nh)] * nct, axis=0)
    o = (oc * lax.rsqrt(var + EPS) * norm * _silu(gate)).astype(y_ref.dtype)
    for c in range(nct):
        for h in range(nh):
            y_ref[0, c * CHUNK:(c + 1) * CHUNK, h * B_DV:(h + 1) * B_DV] = o[c * nh + h]


def _retention_tables():
    log_gamma = jnp.log1p(-jnp.exp2(-5.0 - jnp.arange(B_HEADS, dtype=F32)))
    idx = jnp.arange(CHUNK, dtype=F32)
    intra = jnp.exp(log_gamma[:, None, None] * jnp.abs(idx[:, None] - idx[None, :]))
    read = jnp.broadcast_to(jnp.exp(log_gamma[:, None] * (idx + 1.0))[:, :, None], (B_HEADS, CHUNK, B_DV))
    write = jnp.broadcast_to(jnp.exp(log_gamma[:, None] * (CHUNK - 1.0 - idx))[:, :, None], (B_HEADS, CHUNK, B_DK))
    cdec = jnp.broadcast_to(jnp.exp(log_gamma * CHUNK)[:, None, None], (B_HEADS, 1, B_DV))
    return intra, read, write, cdec


def _rope_tables(s):
    pos = jnp.arange(s, dtype=F32)
    inv_freq = ROPE_BASE ** (-jnp.arange(0, B_DK, 2, dtype=F32) / B_DK)
    ang = pos[:, None] * inv_freq[None, :]
    cos, sin = jnp.cos(ang), jnp.sin(ang)
    return jnp.concatenate([cos, cos], axis=-1), jnp.concatenate([-sin, sin], axis=-1)


def _mixer_b(x, w, cos2, sin2, tables, norm, ts):
    bsz, s, d = x.shape
    intra, read, write, cdec = tables
    full = lambda shape: pl.BlockSpec(shape, lambda b, j: (0,) * len(shape))
    return pl.pallas_call(
        functools.partial(_mixer_b_kernel, ts=ts),
        grid=(bsz, s // ts),
        in_specs=[pl.BlockSpec((1, ts, d), lambda b, j: (b, j, 0)),
                  full((d, 4 * B_W)),
                  pl.BlockSpec((ts, B_DK), lambda b, j: (j, 0)), pl.BlockSpec((ts, B_DK), lambda b, j: (j, 0)),
                  full(intra.shape), full(read.shape), full(write.shape), full(cdec.shape), full((1, B_W))],
        out_specs=pl.BlockSpec((1, ts, B_W), lambda b, j: (b, j, 0)),
        out_shape=jax.ShapeDtypeStruct((bsz, s, B_W), BF16),
        scratch_shapes=[pltpu.VMEM((B_HEADS, B_DK, B_DV), F32)],
        compiler_params=pltpu.CompilerParams(dimension_semantics=("parallel", "arbitrary"),
                                             vmem_limit_bytes=VMEM_LIMIT),
        name="mixer_b",
    )(x, w, cos2, sin2, intra, read, write, cdec, norm)


def _mixer_c_kernel(x_ref, w_ref, conv_ref, convb_ref, dtb_ref, alog_ref, dskip_ref, norm_ref, expand_ref,
                    y_ref, buf_ref, state_ref, *, ts):
    nct = ts // CHUNK

    @pl.when(pl.program_id(1) == 0)
    def _():
        buf_ref[0:8, :] = jnp.zeros((8, C_XBC_W), F32)
        state_ref[...] = jnp.zeros_like(state_ref)

    xb = x_ref[0].astype(BF16)
    proj = _dot(xb, w_ref[...])
    z = proj[:, :C_INNER]
    xbc = _silu(_causal_conv(buf_ref, proj[:, C_INNER:C_INNER + C_XBC_W], conv_ref, ts, 0, C_XBC_W)
                + convb_ref[...])
    xs = xbc[:, :C_INNER]
    bm = xbc[:, C_INNER:C_INNER + C_GROUPS * C_STATE]
    cm = xbc[:, C_INNER + C_GROUPS * C_STATE:]
    dt = _softplus(proj[:, C_INNER + C_XBC_W:] + dtb_ref[...])
    lstep = dt * (-jnp.exp(alog_ref[...]))
    lc = _chunk_cumsum(lstep)
    expand = expand_ref[...]
    lce = _dot_sel_rhs(lc, expand)
    dte = _dot(dt.astype(BF16), expand)
    ng = C_GROUPS

    def stack(per_group):
        parts = [per_group(g) for g in range(ng)]
        return jnp.concatenate([parts[g][c:c + 1] for c in range(nct) for g in range(ng)], axis=0)

    def group_lanes(a, width=C_GROUP_W):
        return stack(lambda g: a[:, g * width:(g + 1) * width].reshape(nct, CHUNK, width))

    def group_rows(ref):
        return jnp.concatenate([ref[:, g * C_GROUP_W:(g + 1) * C_GROUP_W][None] for g in range(ng)] * nct, axis=0)

    lg = group_lanes(lce)
    dg = group_lanes(dte)
    xg = group_lanes(xs)
    zg = group_lanes(z)
    bmg = group_lanes(bm, C_STATE).astype(BF16)
    cmg = group_lanes(cm, C_STATE).astype(BF16)
    heads_per_group = C_HEADS // ng

    def row_form(a):
        at = jnp.swapaxes(a.reshape(nct, CHUNK, LANES), 1, 2)
        return stack(lambda g: jnp.concatenate(
            [at[:, g * heads_per_group + h:g * heads_per_group + h + 1, :] for h in range(heads_per_group)], axis=-1))

    lc_row = row_form(lc)
    dt_row = row_form(dt)
    bm_t = jnp.concatenate([bmg] * (C_GROUP_W // CHUNK), axis=1)
    cbt = jnp.einsum('bin,bmn->bim', cmg, bm_t, preferred_element_type=F32)
    wm = (cbt * jnp.exp(-jnp.abs(lg - lc_row)) * dt_row).astype(BF16)
    lane = lax.broadcasted_iota(jnp.int32, (1, 1, LANES), 2)
    left = lane < C_HEAD_DIM
    pair_out = []
    for p in range(C_GROUP_W // LANES):
        xp = xg[:, :, p * LANES:(p + 1) * LANES]
        bd = jnp.concatenate([jnp.where(left, xp, 0.0), jnp.where(left, 0.0, xp)], axis=1)
        pair_out.append(_bmm(wm[:, :, p * LANES:(p + 1) * LANES], bd.astype(BF16)))
    y = jnp.concatenate(pair_out, axis=-1)
    llast = lg[:, CHUNK - 1:CHUNK, :]
    xw = (xg * (jnp.exp(llast - lg) * dg)).astype(BF16)
    kv = _bmm(jnp.swapaxes(bmg, 1, 2), xw)
    sdec = jnp.exp(llast)
    st = state_ref[...]
    prev = []
    for ch in range(nct):
        sl = slice(ch * ng, (ch + 1) * ng)
        prev.append(st.astype(BF16))
        st = sdec[sl] * st + kv[sl]
    state_ref[...] = st
    y = y + _bmm(cmg, jnp.concatenate(prev, axis=0)) * jnp.exp(lg)
    y = (y + group_rows(dskip_ref) * xg) * _silu(zg)
    y = (y * lax.rsqrt(jnp.mean(y * y, axis=-1, keepdims=True) + EPS) * group_rows(norm_ref)).astype(y_ref.dtype)
    for ch in range(nct):
        for g in range(ng):
            y_ref[0, ch * CHUNK:(ch + 1) * CHUNK, g * C_GROUP_W:(g + 1) * C_GROUP_W] = y[ch * ng + g]


def _mixer_c(x, w, conv, convb, dtb, alog, dskip, norm, expand, ts):
    bsz, s, d = x.shape
    wcols = w.shape[1]
    full = lambda shape: pl.BlockSpec(shape, lambda b, j: (0,) * len(shape))
    return pl.pallas_call(
        functools.partial(_mixer_c_kernel, ts=ts),
        grid=(bsz, s // ts),
        in_specs=[pl.BlockSpec((1, ts, d), lambda b, j: (b, j, 0)),
                  full((d, wcols)), full((CONV_K, C_XBC_W)), full((1, C_XBC_W)), full((1, LANES)),
                  full((1, LANES)), full((1, C_INNER)), full((1, C_INNER)), full((LANES, C_INNER))],
        out_specs=pl.BlockSpec((1, ts, C_INNER), lambda b, j: (b, j, 0)),
        out_shape=jax.ShapeDtypeStruct((bsz, s, C_INNER), BF16),
        scratch_shapes=[pltpu.VMEM((8 + ts, C_XBC_W), F32), pltpu.VMEM((C_GROUPS, C_STATE, C_GROUP_W), F32)],
        compiler_params=pltpu.CompilerParams(dimension_semantics=("parallel", "arbitrary"),
                                             vmem_limit_bytes=VMEM_LIMIT),
        name="mixer_c",
    )(x, w, conv, convb, dtb, alog, dskip, norm, expand)


def _layernorm(h, g, b):
    mu = jnp.mean(h, axis=-1, keepdims=True)
    hc = h - mu
    var = jnp.mean(hc * hc, axis=-1, keepdims=True)
    return hc * lax.rsqrt(var + EPS) * g + b


def _merge_kernel(x_ref, ya_ref, yb_ref, yc_ref, wg_ref, bg_ref, wa_ref, wb_ref, wc_ref, wo_ref,
                  lng_ref, lnb_ref, wrh_ref, wrl_ref, br_ref, x1_ref, x1b_ref, ids_ref, wts_ref, cnt_ref,
                  run_ref):
    @pl.when(pl.program_id(0) == 0)
    def _():
        run_ref[...] = jnp.zeros_like(run_ref)

    x = x_ref[...]
    xb = x.astype(BF16)
    merged = None
    for i, (y_ref, w_ref) in enumerate(((ya_ref, wa_ref), (yb_ref, wb_ref), (yc_ref, wc_ref))):
        gate = _sigmoid(_dot(xb, wg_ref[:, i * D_MODEL:(i + 1) * D_MODEL]) + bg_ref[i:i + 1, :])
        term = gate * _dot(y_ref[...], w_ref[...])
        merged = term if merged is None else merged + term
    mix = _dot(merged.astype(BF16), wo_ref[...])
    x1 = _layernorm(DEEPNORM_ALPHA * x + mix, lng_ref[...], lnb_ref[...])
    x1_ref[...] = x1
    x1b_ref[...] = x1.astype(BF16)

    xh, xl = _split2(x1)
    logits = _dot(xh, wrh_ref[...]) + _dot(xl, wrh_ref[...]) + _dot(xh, wrl_ref[...]) + br_ref[...]
    tm = logits.shape[0]
    lane = lax.broadcasted_iota(jnp.int32, logits.shape, 1)
    neg = jnp.float32(-jnp.inf)
    gmask = lane < N_GROUPS
    gl = jnp.where(gmask, logits, neg)
    gmax = jnp.max(gl, axis=-1, keepdims=True)
    gidx = jnp.min(jnp.where(gmask & (gl == gmax), lane, LANES), axis=-1, keepdims=True)
    gp = 1.0 / jnp.sum(jnp.where(gmask, jnp.exp(gl - gmax), 0.0), axis=-1, keepdims=True)
    emask = (lane >= N_GROUPS) & (lane < N_GROUPS + N_EXPERTS) & ((lane - N_GROUPS) // EXPERTS_PER_GROUP == gidx)
    el = jnp.where(emask, logits, neg)
    v1 = jnp.max(el, axis=-1, keepdims=True)
    i1 = jnp.min(jnp.where(emask & (el == v1), lane, LANES), axis=-1, keepdims=True)
    el2 = jnp.where(lane == i1, neg, el)
    v2 = jnp.max(el2, axis=-1, keepdims=True)
    i2 = jnp.min(jnp.where(emask & (lane != i1) & (el2 == v2), lane, LANES), axis=-1, keepdims=True)
    e21 = jnp.exp(v2 - v1)
    w1 = gp / (1.0 + e21)
    w2 = gp * e21 / (1.0 + e21)

    hot1 = lane == i1
    hot2 = lane == i2
    hot = jnp.where(hot1 | hot2, 1.0, 0.0)
    r = lax.broadcasted_iota(jnp.int32, (tm, tm), 0)
    c = lax.broadcasted_iota(jnp.int32, (tm, tm), 1)
    before = _dot(jnp.where(c < r, 1.0, 0.0).astype(BF16), hot.astype(BF16)) + run_ref[0:1, :]
    rank1 = jnp.sum(jnp.where(hot1, before, 0.0), axis=-1, keepdims=True)
    rank2 = jnp.sum(jnp.where(hot2, before, 0.0), axis=-1, keepdims=True)
    run_ref[0:1, :] = run_ref[0:1, :] + jnp.sum(hot, axis=0, keepdims=True)
    cnt_ref[...] = jnp.broadcast_to(run_ref[0:1, :], cnt_ref.shape)

    ids = jnp.where(lane == 0, (i1 - N_GROUPS).astype(F32), jnp.where(lane == 1, (i2 - N_GROUPS).astype(F32),
                    jnp.where(lane == 2, rank1, jnp.where(lane == 3, rank2, 0.0))))
    ids_ref[...] = jnp.transpose(ids)[0:8, :].astype(jnp.int32)
    wts_ref[...] = jnp.where(lane == 0, w1, jnp.where(lane == 1, w2, 0.0))


def _merge(x2d, ya, yb, yc, wg, bg, wa, wb, wc, wo, lng, lnb, wrh, wrl, br, tm):
    t, d = x2d.shape
    full = lambda shape: pl.BlockSpec(shape, lambda i: (0,) * len(shape))
    row = lambda w: pl.BlockSpec((tm, w), lambda i: (i, 0))
    return pl.pallas_call(
        _merge_kernel,
        grid=(t // tm,),
        in_specs=[row(d), row(A_V_W), row(B_W), row(C_INNER),
                  full(wg.shape), full(bg.shape), full(wa.shape), full(wb.shape), full(wc.shape), full(wo.shape),
                  full((1, d)), full((1, d)), full(wrh.shape), full(wrl.shape), full((1, LANES))],
        out_specs=[row(d), row(d), pl.BlockSpec((8, tm), lambda i: (0, i)), row(LANES), full((8, LANES))],
        out_shape=[jax.ShapeDtypeStruct((t, d), F32), jax.ShapeDtypeStruct((t, d), BF16),
                   jax.ShapeDtypeStruct((8, t), jnp.int32), jax.ShapeDtypeStruct((t, LANES), F32),
                   jax.ShapeDtypeStruct((8, LANES), F32)],
        scratch_shapes=[pltpu.VMEM((8, LANES), F32)],
        compiler_params=pltpu.CompilerParams(dimension_semantics=("arbitrary",), vmem_limit_bytes=VMEM_LIMIT),
        name="merge_ln_router",
    )(x2d, ya, yb, yc, wg, bg, wa, wb, wc, wo, lng, lnb, wrh, wrl, br)


def _experts_kernel(te_ref, nv_ref, xs_ref, wg_ref, wu_ref, wd_ref, *rest, first_tile):
    ys_ref, wgb_ref, wub_ref, wdb_ref = rest[-4:]
    i = pl.program_id(0)
    tile = i + first_tile

    @pl.when((i == 0) | (te_ref[tile] != te_ref[jnp.maximum(tile - 1, 0)]))
    def _():
        wgb_ref[...] = wg_ref[0, 0].astype(BF16)
        wub_ref[...] = wu_ref[0, 0].astype(BF16)
        wdb_ref[...] = wd_ref[0, 0].astype(BF16)

    @pl.when(tile < nv_ref[0])
    def _():
        xs = xs_ref[...]
        h = _silu(_dot(xs, wgb_ref[...])) * _dot(xs, wub_ref[...])
        ys_ref[...] = _dot(h.astype(BF16), wdb_ref[...]).astype(ys_ref.dtype)

    @pl.when(tile >= nv_ref[0])
    def _():
        ys_ref[...] = jnp.zeros_like(ys_ref)


def _experts(tile_expert, n_valid, xs, wg, wu, wd, layer, tm, first_tile, total_rows, ys_prev):
    rows, d = xs.shape
    wspec = lambda shape: pl.BlockSpec((1, 1) + shape, lambda i, te, nv: (layer, te[i + first_tile], 0, 0))
    in_specs = [pl.BlockSpec((tm, d), lambda i, te, nv: (i, 0)),
                wspec((d, D_EXPERT)), wspec((d, D_EXPERT)), wspec((D_EXPERT, d))]
    args = [tile_expert, n_valid, xs, wg, wu, wd]
    aliases = {}
    if ys_prev is not None:
        in_specs.append(pl.BlockSpec(memory_space=pl.ANY))
        aliases = {len(args): 0}
        args.append(ys_prev)
    grid_spec = pltpu.PrefetchScalarGridSpec(
        num_scalar_prefetch=2,
        grid=(rows // tm,),
        in_specs=in_specs,
        out_specs=pl.BlockSpec((tm, d), lambda i, te, nv: (i + first_tile, 0)),
        scratch_shapes=[pltpu.VMEM((d, D_EXPERT), BF16), pltpu.VMEM((d, D_EXPERT), BF16),
                        pltpu.VMEM((D_EXPERT, d), BF16)],
    )
    return pl.pallas_call(
        functools.partial(_experts_kernel, first_tile=first_tile),
        grid_spec=grid_spec,
        out_shape=jax.ShapeDtypeStruct((total_rows, d), BF16),
        input_output_aliases=aliases,
        compiler_params=pltpu.CompilerParams(dimension_semantics=("arbitrary",), vmem_limit_bytes=VMEM_LIMIT),
        name="grouped_experts",
    )(*args)


def _combine_kernel(x_ref, g_ref, wts_ref, lng_ref, lnb_ref, *rest):
    o_ref = rest[-1]
    wts = wts_ref[...]
    ffn = wts[:, 0:1] * g_ref[0].astype(F32) + wts[:, 1:2] * g_ref[1].astype(F32)
    o_ref[...] = _layernorm(DEEPNORM_ALPHA * x_ref[...] + ffn, lng_ref[...], lnb_ref[...])


def _combine(x1, g, wts, lng, lnb, tm, first_tile, out_prev):
    t, d = x1.shape
    row = lambda w: pl.BlockSpec((tm, w), lambda i: (i + first_tile, 0))
    vec = pl.BlockSpec((1, d), lambda i: (0, 0))
    in_specs = [row(d), pl.BlockSpec((2, tm, d), lambda i: (0, i, 0)), row(LANES), vec, vec]
    args = [x1, g, wts, lng, lnb]
    aliases = {}
    if out_prev is not None:
        in_specs.append(pl.BlockSpec(memory_space=pl.ANY))
        aliases = {len(args): 0}
        args.append(out_prev)
    return pl.pallas_call(
        _combine_kernel,
        grid=(g.shape[1] // tm,),
        in_specs=in_specs,
        out_specs=row(d),
        out_shape=jax.ShapeDtypeStruct((t, d), F32),
        input_output_aliases=aliases,
        compiler_params=pltpu.CompilerParams(dimension_semantics=("parallel",), vmem_limit_bytes=VMEM_LIMIT),
        name="combine_ln",
    )(*args)


def _pad_lanes(v, width=LANES):
    v = v.reshape(1, -1).astype(F32)
    return jnp.pad(v, ((0, 0), (0, width - v.shape[1])))


def _rows(a, idx):
    return a.at[idx].get(mode="promise_in_bounds")


def _dispatch(ids, counts, t, tm):
    n_pairs = 2 * t
    n_tiles = n_pairs // tm + N_EXPERTS
    n_rows = n_tiles * tm
    e1, e2, rank1, rank2 = ids[0], ids[1], ids[2], ids[3]
    padded = ((counts + tm - 1) // tm) * tm
    pend = jnp.cumsum(padded)
    pstart = pend - padded
    pos = jnp.concatenate([_rows(pstart, e1) + rank1, _rows(pstart, e2) + rank2])
    count_le = lambda table, v: jnp.sum((table[None, :] <= v[:, None]).astype(jnp.int32), axis=1)
    span = 2 * n_pairs
    tok2 = 2 * jnp.arange(t, dtype=jnp.int32)
    slot = jnp.arange(n_rows - n_pairs, dtype=jnp.int32)
    slot_e = count_le(jnp.cumsum(padded - counts), slot)
    keys = jnp.concatenate([e1 * span + tok2, e2 * span + tok2 + 1, slot_e * span + n_pairs + slot % n_pairs])
    low = lax.sort(keys) % span
    row_tok = jnp.where(low < n_pairs, low // 2, jnp.arange(n_rows, dtype=jnp.int32) % t)
    n_valid = (pend[-1] // tm).astype(jnp.int32).reshape(1)
    tile_start = jnp.arange(n_tiles, dtype=jnp.int32) * tm
    tile_e = jnp.minimum(count_le(pend, tile_start), count_le(pend, pend[-1:] - 1))
    return row_tok, pos, tile_e, n_valid


def _layer(x, p, consts, seq_tile, token_tile, expert_tile):
    bsz, s, d = x.shape
    t = bsz * s
    ya = _mixer_a(x, p["w_a"], p["conv_a"], p["a_log_a"], p["dt_bias_a"], p["norm_a"], consts["expand_a"],
                  seq_tile // A_SEQS, A_SEQS)
    yb = _mixer_b(x, p["w_b"], consts["cos2"], consts["sin2"], consts["ret"], p["norm_b"], seq_tile)
    yc = _mixer_c(x, p["w_c"], p["conv_c"], p["conv_bias_c"], p["dt_bias_c"], p["a_log_c"], p["d_skip_c"],
                  p["norm_c"], consts["expand"], seq_tile)
    x1, x1b, ids, wts, cnt = _merge(x.reshape(t, d), ya.reshape(t, -1), yb.reshape(t, -1), yc.reshape(t, -1),
                                    p["w_gate_in"], p["b_gate"], p["w_branch_a"], p["w_branch_b"],
                                    p["w_branch_c"], p["w_out"], p["ln1_g"], p["ln1_b"], p["w_router_hi"],
                                    p["w_router_lo"], p["b_router"], token_tile)
    counts = cnt[0, N_GROUPS:N_GROUPS + N_EXPERTS].astype(jnp.int32)
    row_tok, pos, tile_e, n_valid = _dispatch(ids, counts, t, expert_tile)
    n_rows = row_tok.shape[0]
    rows_c = n_rows // DISPATCH_CHUNKS
    ys = None
    for k in range(DISPATCH_CHUNKS):
        xs = _rows(x1b, row_tok[k * rows_c:(k + 1) * rows_c])
        ys = _experts(tile_e, n_valid, xs, p["w_gate_e"], p["w_up_e"], p["w_down_e"], p["layer"], expert_tile,
                      k * (rows_c // expert_tile), n_rows, ys)
    pos = pos.reshape(2, t)
    t_c = t // DISPATCH_CHUNKS
    x2 = None
    for k in range(DISPATCH_CHUNKS):
        g = _rows(ys, pos[:, k * t_c:(k + 1) * t_c].reshape(-1)).reshape(2, t_c, d)
        x2 = _combine(x1, g, wts, p["ln2_g"], p["ln2_b"], token_tile, k * (t_c // token_tile), x2)
    return x2.reshape(bsz, s, d)


def _layer_params(l, w_in, conv_a, a_log_a, dt_bias_a, norm_a, norm_b, conv_c, conv_bias_c, dt_bias_c,
                  a_log_c, d_skip_c, norm_c, b_gate, w_branch_a, w_branch_b, w_branch_c, w_out,
                  ln1_g, ln1_b, w_router_group, b_router_group, w_router_expert, b_router_expert,
                  w_gate_e, w_up_e, w_down_e, ln2_g, ln2_b):
    o = IN_OFFS
    w = w_in[l]
    d = w.shape[0]
    zpad = lambda n: jnp.zeros((d, n), F32)
    w_a = jnp.concatenate([w[:, o[0]:o[2]], w[:, o[2]:o[3]], zpad(LANES - A_HEADS),
                           w[:, o[3]:o[4]], zpad(LANES - A_HEADS)], axis=1).astype(BF16)
    w_b = w[:, o[4]:o[8]].astype(BF16)
    w_c = jnp.concatenate([w[:, o[8]:o[10]], w[:, o[10]:o[11]], zpad(LANES - C_HEADS)], axis=1).astype(BF16)
    w_router = jnp.concatenate([w_router_group[l], w_router_expert[l],
                                zpad(LANES - N_GROUPS - N_EXPERTS)], axis=1)
    b_router = _pad_lanes(jnp.concatenate([b_router_group[l], b_router_expert[l]]))
    return {
        "w_a": w_a, "w_b": w_b, "w_c": w_c, "w_gate_in": w[:, o[11]:o[12]].astype(BF16),
        "conv_a": conv_a[l], "a_log_a": _pad_lanes(a_log_a[l]), "dt_bias_a": _pad_lanes(dt_bias_a[l]),
        "norm_a": norm_a[l].reshape(1, -1), "norm_b": norm_b[l].reshape(1, -1),
        "conv_c": conv_c[l], "conv_bias_c": conv_bias_c[l].reshape(1, -1),
        "dt_bias_c": _pad_lanes(dt_bias_c[l]), "a_log_c": _pad_lanes(a_log_c[l]),
        "d_skip_c": jnp.repeat(d_skip_c[l], C_HEAD_DIM).reshape(1, -1), "norm_c": norm_c[l].reshape(1, -1),
        "b_gate": b_gate[l],
        "w_branch_a": w_branch_a[l].astype(BF16), "w_branch_b": w_branch_b[l].astype(BF16),
        "w_branch_c": w_branch_c[l].astype(BF16), "w_out": w_out[l].astype(BF16),
        "ln1_g": ln1_g[l].reshape(1, -1), "ln1_b": ln1_b[l].reshape(1, -1),
        "w_router_hi": w_router.astype(BF16),
        "w_router_lo": (w_router - w_router.astype(BF16).astype(F32)).astype(BF16), "b_router": b_router,
        "w_gate_e": w_gate_e, "w_up_e": w_up_e, "w_down_e": w_down_e, "layer": l,
        "ln2_g": ln2_g[l].reshape(1, -1), "ln2_b": ln2_b[l].reshape(1, -1),
    }


def _head_expand(heads, width):
    head_of_lane = jnp.arange(heads * width) // width
    return (jnp.arange(LANES)[:, None] == head_of_lane[None, :]).astype(BF16)


def _forward(x, params, seq_tile=SEQ_TILE, token_tile=TOKEN_TILE, expert_tile=EXPERT_TILE):
    s = x.shape[1]
    cos2, sin2 = _rope_tables(s)
    consts = {"cos2": cos2, "sin2": sin2, "ret": _retention_tables(),
              "expand": _head_expand(C_HEADS, C_HEAD_DIM), "expand_a": _head_expand(A_HEADS, CHUNK)}
    for l in range(DEPTH):
        x = _layer(x, _layer_params(l, *params), consts, seq_tile, token_tile, expert_tile)
    return x


def kernel(x, w_in, conv_a, a_log_a, dt_bias_a, norm_a, norm_b, conv_c, conv_bias_c, dt_bias_c, a_log_c, d_skip_c, norm_c, b_gate, w_branch_a, w_branch_b, w_branch_c, w_out, ln1_g, ln1_b, w_router_group, b_router_group, w_router_expert, b_router_expert, w_gate_e, w_up_e, w_down_e, ln2_g, ln2_b):
    params = (w_in, conv_a, a_log_a, dt_bias_a, norm_a, norm_b, conv_c, conv_bias_c, dt_bias_c, a_log_c,
              d_skip_c, norm_c, b_gate, w_branch_a, w_branch_b, w_branch_c, w_out, ln1_g, ln1_b,
              w_router_group, b_router_group, w_router_expert, b_router_expert, w_gate_e, w_up_e, w_down_e,
              ln2_g, ln2_b)
    return _forward(x, params)
```

```python
import functools
import math

import jax
import jax.numpy as jnp
from jax import lax
from jax.experimental import pallas as pl
from jax.experimental.pallas import tpu as pltpu

F32 = jnp.float32
BF16 = jnp.bfloat16

D_MODEL = 1024
DEPTH = 2
CHUNK = 64
CONV_K = 4
EPS = 1e-6
LANES = 128
A_HEADS = 4
A_DK = 128
A_DV = 128
A_QKV_W = 3 * A_HEADS * A_DK
A_V_W = A_HEADS * A_DV
B_HEADS = 4
B_DK = 128
B_DV = 128
B_W = B_HEADS * B_DK
ROPE_BASE = 10000.0
C_INNER = 1024
C_HEAD_DIM = 64
C_HEADS = 16
C_GROUPS = 2
C_STATE = 128
C_XBC_W = C_INNER + 2 * C_GROUPS * C_STATE
C_GROUP_W = C_INNER // C_GROUPS
N_BRANCH = 3
N_GROUPS = 4
EXPERTS_PER_GROUP = 8
N_EXPERTS = 32
D_EXPERT = 512
DEEPNORM_ALPHA = (2 * DEPTH) ** 0.25

IN_SIZES = (A_QKV_W, A_V_W, A_HEADS, A_HEADS, B_W, B_W, B_W, B_W, C_INNER, C_XBC_W, C_HEADS,
            N_BRANCH * D_MODEL)
IN_OFFS = tuple(int(sum(IN_SIZES[:i])) for i in range(len(IN_SIZES) + 1))

SEQ_TILE = 512
A_SEQS = 4
TOKEN_TILE = 512
EXPERT_TILE = 512
DISPATCH_CHUNKS = 4
VMEM_LIMIT = 56 * 1024 * 1024


def _dot(a, b):
    return jnp.dot(a, b, preferred_element_type=F32)


def _split2(a):
    hi = a.astype(BF16)
    return hi, (a - hi.astype(F32)).astype(BF16)


def _bmm(a, b):
    return jnp.einsum('cij,cjk->cik', a, b, preferred_element_type=F32)


def _dot_sel_rhs(a, sel):
    hi, lo = _split2(a)
    return _dot(hi, sel) + _dot(lo, sel)


def _sigmoid(x):
    return 0.5 * jnp.tanh(0.5 * x) + 0.5


def _silu(x):
    h = 0.5 * x
    return h + h * jnp.tanh(h)


def _softplus(x):
    return jnp.maximum(x, 0.0) + jnp.log(1.0 + jnp.exp(-jnp.abs(x)))


def _chunk_cumsum(a):
    ts, lanes = a.shape
    nct = ts // CHUNK
    r = lax.broadcasted_iota(jnp.int32, (nct, CHUNK, CHUNK), 1)
    c = lax.broadcasted_iota(jnp.int32, (nct, CHUNK, CHUNK), 2)
    tri = jnp.where(c <= r, 1.0, 0.0).astype(BF16)
    hi, lo = _split2(a.reshape(nct, CHUNK, lanes))
    return (_bmm(tri, hi) + _bmm(tri, lo)).reshape(ts, lanes)


def _causal_conv(buf_ref, pre, conv_ref, ts, c0, c1):
    buf_ref[8:8 + ts, c0:c1] = pre
    acc = buf_ref[pl.ds(8 - (CONV_K - 1), ts), c0:c1] * conv_ref[0:1, c0:c1]
    for k in range(1, CONV_K):
        acc = acc + buf_ref[pl.ds(8 - (CONV_K - 1) + k, ts), c0:c1] * conv_ref[k:k + 1, c0:c1]
    buf_ref[8 - (CONV_K - 1):8, c0:c1] = buf_ref[8 + ts - (CONV_K - 1):8 + ts, c0:c1]
    return acc


def _mixer_a_kernel(x_ref, w_ref, conv_ref, alog_ref, dtb_ref, norm_ref, e64_ref, y_ref, buf_ref, state_ref,
                    *, ts, nseq):
    nct_seq = ts // CHUNK
    nct = nseq * nct_seq

    @pl.when(pl.program_id(1) == 0)
    def _():
        buf_ref[:, 0:8, :] = jnp.zeros((nseq, 8, A_QKV_W), F32)
        state_ref[...] = jnp.zeros_like(state_ref)

    xb = x_ref[...].reshape(nseq * ts, x_ref.shape[2]).astype(BF16)
    nh = A_HEADS
    qk_w = nh * A_DK

    def conv_cols(c0, c1):
        pre = _dot(xb, w_ref[:, c0:c1])
        return _silu(jnp.concatenate(
            [_causal_conv(buf_ref.at[si], pre[si * ts:(si + 1) * ts], conv_ref, ts, c0, c1) for si in range(nseq)],
            axis=0))

    small = _dot(xb, w_ref[:, 2048:2048 + 2 * LANES])
    a_raw = small[:, :LANES]
    b_raw = small[:, LANES:]
    kf = conv_cols(qk_w, 2 * qk_w)
    beta = _sigmoid(b_raw)
    g = -jnp.exp(alog_ref[...]) * _softplus(a_raw + dtb_ref[...])
    gc = _chunk_cumsum(g)
    gc3 = gc.reshape(nct, CHUNK, LANES)
    beta3 = beta.reshape(nct, CHUNK, LANES)
    gct3 = jnp.swapaxes(gc3, 1, 2)
    wide = nh * CHUNK

    def stack(per_head):
        parts = [per_head(h) for h in range(nh)]
        return jnp.concatenate([parts[h][c:c + 1] for c in range(nct) for h in range(nh)], axis=0)

    def stack_heads(a, width):
        return stack(lambda h: a[:, h * width:(h + 1) * width].reshape(nct, CHUNK, width))

    k = stack_heads(kf, A_DK)
    gcol = stack(lambda h: gc3[:, :, h:h + 1])
    bcol = stack(lambda h: beta3[:, :, h:h + 1])
    glast = stack(lambda h: gc3[:, CHUNK - 1:CHUNK, h:h + 1])
    k = k * lax.rsqrt(jnp.sum(k * k, axis=-1, keepdims=True) + EPS)

    row_w = lax.broadcasted_iota(jnp.int32, (CHUNK, wide), 0)
    col_w = lax.broadcasted_iota(jnp.int32, (CHUNK, wide), 1)
    strict = (col_w % CHUNK < row_w)[None]
    eye = jnp.where(col_w % CHUNK == row_w, 1.0, 0.0).astype(F32)[None]
    blk_r = lax.broadcasted_iota(jnp.int32, (wide, wide), 0) // CHUNK
    blk_c = lax.broadcasted_iota(jnp.int32, (wide, wide), 1) // CHUNK
    same_head = (blk_r == blk_c)[None]

    def block_diag(m):
        tiled = jnp.concatenate([m] * nh, axis=1)
        return jnp.where(same_head, tiled, jnp.zeros_like(tiled))

    gcol_w = _dot_sel_rhs(gc, e64_ref[...]).reshape(nct, CHUNK, wide)
    beta_w = _dot(beta.astype(BF16), e64_ref[...]).reshape(nct, CHUNK, wide)
    grow_w = jnp.concatenate([gct3[:, h:h + 1, :] for h in range(nh)], axis=-1)
    decay = jnp.where(strict, jnp.exp(jnp.where(strict, gcol_w - grow_w, 0.0)), 0.0)
    kn = jnp.concatenate([k[c * nh:(c + 1) * nh].reshape(1, wide, A_DK) for c in range(nct)], axis=0)
    kn = kn.astype(BF16)
    k_nat = jnp.concatenate([jnp.concatenate([k[c * nh + h][None] for c in range(nct)], axis=0)
                             for h in range(nh)], axis=-1).astype(BF16)
    head_r = lax.broadcasted_iota(jnp.int32, (wide, qk_w), 0) // CHUNK
    head_c = lax.broadcasted_iota(jnp.int32, (wide, qk_w), 1) // A_DK
    k_bd = jnp.where((head_r == head_c)[None], jnp.concatenate([kn] * nh, axis=-1), jnp.zeros((), BF16))
    amat = beta_w * jnp.einsum('bid,bjd->bij', k_nat, k_bd, preferred_element_type=F32) * decay
    half = qk_w // 2
    side_work = [lambda: conv_cols(2 * qk_w, 2 * qk_w + half), lambda: conv_cols(2 * qk_w + half, 3 * qk_w),
                 lambda: conv_cols(0, half), lambda: conv_cols(half, qk_w),
                 lambda: _dot(xb, w_ref[:, A_QKV_W:A_QKV_W + A_V_W])]
    side = []
    xinv = eye - amat
    pw = amat.astype(BF16)
    pw_bd = block_diag(pw)
    for step in range(5):
        pw = _bmm(pw, pw_bd).astype(BF16)
        pw_bd = block_diag(pw)
        side.append(side_work[step]())
        xinv = xinv + _bmm(xinv.astype(BF16), pw_bd)
    ah, al = _split2(amat)
    xh, xl = _split2(xinv)
    xh_bd = block_diag(xh)
    resid = eye - xinv - (_bmm(ah, xh_bd) + _bmm(ah, block_diag(xl)) + _bmm(al, xh_bd))
    xinv = xinv + _bmm(xh, block_diag(resid.astype(BF16)))
    v = stack_heads(jnp.concatenate(side[0:2], axis=-1), A_DV)
    q = stack_heads(jnp.concatenate(side[2:4], axis=-1), A_DK)
    q = q * lax.rsqrt(jnp.sum(q * q, axis=-1, keepdims=True) + EPS) * (A_DK ** -0.5)
    zs = stack_heads(side[4], A_DV)
    rhs = jnp.concatenate([v * bcol, k * (bcol * jnp.exp(gcol))], axis=-1).astype(BF16)
    rhs_rows = jnp.concatenate([rhs[c * nh:(c + 1) * nh].reshape(1, wide, 2 * A_DV) for c in range(nct)], axis=0)
    xb16 = xinv.astype(BF16)
    head_of_col = (col_w // CHUNK)[None]
    sol = stack(lambda h: _bmm(jnp.where(head_of_col == h, xb16, jnp.zeros_like(xb16)), rhs_rows))
    u = sol[:, :, :A_DV]
    wq = jnp.concatenate([sol[:, :, A_DV:], q], axis=1).astype(BF16)
    kend = (k * jnp.exp(glast - gcol)).astype(BF16)
    qk = jnp.einsum('bid,bjd->bij', q.astype(BF16), kend, preferred_element_type=F32).astype(BF16)
    kend_t = jnp.swapaxes(kend, 1, 2)
    dec = jnp.exp(glast)
    s = state_ref[...]

    def step_rows(a, c):
        return jnp.concatenate([a[(si * nct_seq + c) * nh:(si * nct_seq + c + 1) * nh] for si in range(nseq)], axis=0)

    outs = []
    for c in range(nct_seq):
        dec_c = step_rows(dec, c)
        r = _bmm(step_rows(wq, c), s.astype(BF16))
        delta = (step_rows(u, c) - r[:, :CHUNK]).astype(BF16)
        outs.append(dec_c * r[:, CHUNK:] + _bmm(step_rows(qk, c), delta))
        s = dec_c * s + _bmm(step_rows(kend_t, c), delta)
    state_ref[...] = s
    o = jnp.concatenate([outs[c][si * nh:(si + 1) * nh] for si in range(nseq) for c in range(nct_seq)], axis=0)
    o = o * lax.rsqrt(jnp.mean(o * o, axis=-1, keepdims=True) + EPS) * norm_ref[...]
    o = (o * _silu(zs)).astype(y_ref.dtype)
    for si in range(nseq):
        for c in range(nct_seq):
            for h in range(nh):
                y_ref[si, c * CHUNK:(c + 1) * CHUNK, h * A_DV:(h + 1) * A_DV] = o[((si * nct_seq + c) * nh) + h]


def _mixer_a(x, w, conv, alog, dtb, norm, expand, ts, nseq):
    bsz, s, d = x.shape
    wcols = w.shape[1]
    full = lambda shape: pl.BlockSpec(shape, lambda b, j: (0,) * len(shape))
    return pl.pallas_call(
        functools.partial(_mixer_a_kernel, ts=ts, nseq=nseq),
        grid=(bsz // nseq, s // ts),
        in_specs=[pl.BlockSpec((nseq, ts, d), lambda b, j: (b, j, 0)),
                  full((d, wcols)), full((CONV_K, A_QKV_W)), full((1, LANES)), full((1, LANES)),
                  full((1, A_DV)), full(expand.shape)],
        out_specs=pl.BlockSpec((nseq, ts, A_V_W), lambda b, j: (b, j, 0)),
        out_shape=jax.ShapeDtypeStruct((bsz, s, A_V_W), BF16),
        scratch_shapes=[pltpu.VMEM((nseq, 8 + ts, A_QKV_W), F32),
                        pltpu.VMEM((nseq * A_HEADS, A_DK, A_DV), F32)],
        compiler_params=pltpu.CompilerParams(dimension_semantics=("parallel", "arbitrary"),
                                             vmem_limit_bytes=VMEM_LIMIT),
        name="mixer_a",
    )(x, w, conv, alog, dtb, norm, expand)


def _mixer_b_kernel(x_ref, w_ref, cos_ref, sin_ref, intra_ref, read_ref, write_ref, cdec_ref, norm_ref,
                    y_ref, state_ref, *, ts):
    nct = ts // CHUNK

    @pl.when(pl.program_id(1) == 0)
    def _():
        state_ref[...] = jnp.zeros_like(state_ref)

    xb = x_ref[0].astype(BF16)
    proj = _dot(xb, w_ref[...])
    cos = cos_ref[...]
    sin = sin_ref[...]
    nh = B_HEADS

    def stack(per_head):
        parts = [per_head(h) for h in range(nh)]
        return jnp.concatenate([parts[h][c:c + 1] for c in range(nct) for h in range(nh)], axis=0)

    def rope(t):
        return t * cos + pltpu.roll(t, B_DK // 2, 1) * sin

    q = stack(lambda h: rope(proj[:, h * B_DK:(h + 1) * B_DK]).reshape(nct, CHUNK, B_DK)).astype(BF16)
    k = stack(lambda h: (rope(proj[:, B_W + h * B_DK:B_W + (h + 1) * B_DK]) * (B_DK ** -0.5))
              .reshape(nct, CHUNK, B_DK))
    v = stack(lambda h: proj[:, 2 * B_W + h * B_DV:2 * B_W + (h + 1) * B_DV].reshape(nct, CHUNK, B_DV)).astype(BF16)
    gate = stack(lambda h: proj[:, 3 * B_W + h * B_DV:3 * B_W + (h + 1) * B_DV].reshape(nct, CHUNK, B_DV))
    tile_heads = lambda ref: jnp.concatenate([ref[...]] * nct, axis=0)
    scores = jnp.einsum('bid,bjd->bij', q, k.astype(BF16), preferred_element_type=F32) * tile_heads(intra_ref)
    o = _bmm(scores.astype(BF16), v)
    kw_t = jnp.swapaxes((k * tile_heads(write_ref)).astype(BF16), 1, 2)
    kv = _bmm(kw_t, v)
    s = state_ref[...]
    cdec = cdec_ref[...]
    prev = []
    for c in range(nct):
        prev.append(s.astype(BF16))
        s = cdec * s + kv[c * nh:(c + 1) * nh]
    state_ref[...] = s
    o = o + _bmm(q, jnp.concatenate(prev, axis=0)) * tile_heads(read_ref)
    mu = jnp.mean(o, axis=-1, keepdims=True)
    oc = o - mu
    var = jnp.mean(oc * oc, axis=-1, keepdims=True)
    norm = jnp.concatenate([norm_ref[:, h * B_DV:(h + 1) * B_DV][None] for h in range(nh)] * nct, axis=0)
    o = (oc * lax.rsqrt(var + EPS) * norm * _silu(gate)).astype(y_ref.dtype)
    for c in range(nct):
        for h in range(nh):
            y_ref[0, c * CHUNK:(c + 1) * CHUNK, h * B_DV:(h + 1) * B_DV] = o[c * nh + h]


def _retention_tables():
    log_gamma = jnp.log1p(-jnp.exp2(-5.0 - jnp.arange(B_HEADS, dtype=F32)))
    idx = jnp.arange(CHUNK, dtype=F32)
    intra = jnp.exp(log_gamma[:, None, None] * jnp.abs(idx[:, None] - idx[None, :]))
    read = jnp.broadcast_to(jnp.exp(log_gamma[:, None] * (idx + 1.0))[:, :, None], (B_HEADS, CHUNK, B_DV))
    write = jnp.broadcast_to(jnp.exp(log_gamma[:, None] * (CHUNK - 1.0 - idx))[:, :, None], (B_HEADS, CHUNK, B_DK))
    cdec = jnp.broadcast_to(jnp.exp(log_gamma * CHUNK)[:, None, None], (B_HEADS, 1, B_DV))
    return intra, read, write, cdec


def _rope_tables(s):
    pos = jnp.arange(s, dtype=F32)
    inv_freq = ROPE_BASE ** (-jnp.arange(0, B_DK, 2, dtype=F32) / B_DK)
    ang = pos[:, None] * inv_freq[None, :]
    cos, sin = jnp.cos(ang), jnp.sin(ang)
    return jnp.concatenate([cos, cos], axis=-1), jnp.concatenate([-sin, sin], axis=-1)


def _mixer_b(x, w, cos2, sin2, tables, norm, ts):
    bsz, s, d = x.shape
    intra, read, write, cdec = tables
    full = lambda shape: pl.BlockSpec(shape, lambda b, j: (0,) * len(shape))
    return pl.pallas_call(
        functools.partial(_mixer_b_kernel, ts=ts),
        grid=(bsz, s // ts),
        in_specs=[pl.BlockSpec((1, ts, d), lambda b, j: (b, j, 0)),
                  full((d, 4 * B_W)),
                  pl.BlockSpec((ts, B_DK), lambda b, j: (j, 0)), pl.BlockSpec((ts, B_DK), lambda b, j: (j, 0)),
                  full(intra.shape), full(read.shape), full(write.shape), full(cdec.shape), full((1, B_W))],
        out_specs=pl.BlockSpec((1, ts, B_W), lambda b, j: (b, j, 0)),
        out_shape=jax.ShapeDtypeStruct((bsz, s, B_W), BF16),
        scratch_shapes=[pltpu.VMEM((B_HEADS, B_DK, B_DV), F32)],
        compiler_params=pltpu.CompilerParams(dimension_semantics=("parallel", "arbitrary"),
                                             vmem_limit_bytes=VMEM_LIMIT),
        name="mixer_b",
    )(x, w, cos2, sin2, intra, read, write, cdec, norm)


def _mixer_c_kernel(x_ref, w_ref, conv_ref, convb_ref, dtb_ref, alog_ref, dskip_ref, norm_ref, expand_ref,
                    y_ref, buf_ref, state_ref, *, ts):
    nct = ts // CHUNK

    @pl.when(pl.program_id(1) == 0)
    def _():
        buf_ref[0:8, :] = jnp.zeros((8, C_XBC_W), F32)
        state_ref[...] = jnp.zeros_like(state_ref)

    xb = x_ref[0].astype(BF16)
    proj = _dot(xb, w_ref[...])
    z = proj[:, :C_INNER]
    xbc = _silu(_causal_conv(buf_ref, proj[:, C_INNER:C_INNER + C_XBC_W], conv_ref, ts, 0, C_XBC_W)
                + convb_ref[...])
    xs = xbc[:, :C_INNER]
    bm = xbc[:, C_INNER:C_INNER + C_GROUPS * C_STATE]
    cm = xbc[:, C_INNER + C_GROUPS * C_STATE:]
    dt = _softplus(proj[:, C_INNER + C_XBC_W:] + dtb_ref[...])
    lstep = dt * (-jnp.exp(alog_ref[...]))
    lc = _chunk_cumsum(lstep)
    expand = expand_ref[...]
    lce = _dot_sel_rhs(lc, expand)
    dte = _dot(dt.astype(BF16), expand)
    ng = C_GROUPS

    def stack(per_group):
        parts = [per_group(g) for g in range(ng)]
        return jnp.concatenate([parts[g][c:c + 1] for c in range(nct) for g in range(ng)], axis=0)

    def group_lanes(a, width=C_GROUP_W):
        return stack(lambda g: a[:, g * width:(g + 1) * width].reshape(nct, CHUNK, width))

    def group_rows(ref):
        return jnp.concatenate([ref[:, g * C_GROUP_W:(g + 1) * C_GROUP_W][None] for g in range(ng)] * nct, axis=0)

    lg = group_lanes(lce)
    dg = group_lanes(dte)
    xg = group_lanes(xs)
    zg = group_lanes(z)
    bmg = group_lanes(bm, C_STATE).astype(BF16)
    cmg = group_lanes(cm, C_STATE).astype(BF16)
    heads_per_group = C_HEADS // ng

    def row_form(a):
        at = jnp.swapaxes(a.reshape(nct, CHUNK, LANES), 1, 2)
        return stack(lambda g: jnp.concatenate(
            [at[:, g * heads_per_group + h:g * heads_per_group + h + 1, :] for h in range(heads_per_group)], axis=-1))

    lc_row = row_form(lc)
    dt_row = row_form(dt)
    bm_t = jnp.concatenate([bmg] * (C_GROUP_W // CHUNK), axis=1)
    cbt = jnp.einsum('bin,bmn->bim', cmg, bm_t, preferred_element_type=F32)
    wm = (cbt * jnp.exp(-jnp.abs(lg - lc_row)) * dt_row).astype(BF16)
    lane = lax.broadcasted_iota(jnp.int32, (1, 1, LANES), 2)
    left = lane < C_HEAD_DIM
    pair_out = []
    for p in range(C_GROUP_W // LANES):
        xp = xg[:, :, p * LANES:(p + 1) * LANES]
        bd = jnp.concatenate([jnp.where(left, xp, 0.0), jnp.where(left, 0.0, xp)], axis=1)
        pair_out.append(_bmm(wm[:, :, p * LANES:(p + 1) * LANES], bd.astype(BF16)))
    y = jnp.concatenate(pair_out, axis=-1)
    llast = lg[:, CHUNK - 1:CHUNK, :]
    xw = (xg * (jnp.exp(llast - lg) * dg)).astype(BF16)
    kv = _bmm(jnp.swapaxes(bmg, 1, 2), xw)
    sdec = jnp.exp(llast)
    st = state_ref[...]
    prev = []
    for ch in range(nct):
        sl = slice(ch * ng, (ch + 1) * ng)
        prev.append(st.astype(BF16))
        st = sdec[sl] * st + kv[sl]
    state_ref[...] = st
    y = y + _bmm(cmg, jnp.concatenate(prev, axis=0)) * jnp.exp(lg)
    y = (y + group_rows(dskip_ref) * xg) * _silu(zg)
    y = (y * lax.rsqrt(jnp.mean(y * y, axis=-1, keepdims=True) + EPS) * group_rows(norm_ref)).astype(y_ref.dtype)
    for ch in range(nct):
        for g in range(ng):
            y_ref[0, ch * CHUNK:(ch + 1) * CHUNK, g * C_GROUP_W:(g + 1) * C_GROUP_W] = y[ch * ng + g]


def _mixer_c(x, w, conv, convb, dtb, alog, dskip, norm, expand, ts):
    bsz, s, d = x.shape
    wcols = w.shape[1]
    full = lambda shape: pl.BlockSpec(shape, lambda b, j: (0,) * len(shape))
    return pl.pallas_call(
        functools.partial(_mixer_c_kernel, ts=ts),
        grid=(bsz, s // ts),
        in_specs=[pl.BlockSpec((1, ts, d), lambda b, j: (b, j, 0)),
                  full((d, wcols)), full((CONV_K, C_XBC_W)), full((1, C_XBC_W)), full((1, LANES)),
                  full((1, LANES)), full((1, C_INNER)), full((1, C_INNER)), full((LANES, C_INNER))],
        out_specs=pl.BlockSpec((1, ts, C_INNER), lambda b, j: (b, j, 0)),
        out_shape=jax.ShapeDtypeStruct((bsz, s, C_INNER), BF16),
        scratch_shapes=[pltpu.VMEM((8 + ts, C_XBC_W), F32), pltpu.VMEM((C_GROUPS, C_STATE, C_GROUP_W), F32)],
        compiler_params=pltpu.CompilerParams(dimension_semantics=("parallel", "arbitrary"),
                                             vmem_limit_bytes=VMEM_LIMIT),
        name="mixer_c",
    )(x, w, conv, convb, dtb, alog, dskip, norm, expand)


def _layernorm(h, g, b):
    mu = jnp.mean(h, axis=-1, keepdims=True)
    hc = h - mu
    var = jnp.mean(hc * hc, axis=-1, keepdims=True)
    return hc * lax.rsqrt(var + EPS) * g + b


def _merge_kernel(x_ref, ya_ref, yb_ref, yc_ref, wg_ref, bg_ref, wa_ref, wb_ref, wc_ref, wo_ref,
                  lng_ref, lnb_ref, wrh_ref, wrl_ref, br_ref, x1_ref, x1b_ref, ids_ref, wts_ref, cnt_ref,
                  run_ref):
    @pl.when(pl.program_id(0) == 0)
    def _():
        run_ref[...] = jnp.zeros_like(run_ref)

    x = x_ref[...]
    xb = x.astype(BF16)
    merged = None
    for i, (y_ref, w_ref) in enumerate(((ya_ref, wa_ref), (yb_ref, wb_ref), (yc_ref, wc_ref))):
        gate = _sigmoid(_dot(xb, wg_ref[:, i * D_MODEL:(i + 1) * D_MODEL]) + bg_ref[i:i + 1, :])
        term = gate * _dot(y_ref[...], w_ref[...])
        merged = term if merged is None else merged + term
    mix = _dot(merged.astype(BF16), wo_ref[...])
    x1 = _layernorm(DEEPNORM_ALPHA * x + mix, lng_ref[...], lnb_ref[...])
    x1_ref[...] = x1
    x1b_ref[...] = x1.astype(BF16)

    xh, xl = _split2(x1)
    logits = _dot(xh, wrh_ref[...]) + _dot(xl, wrh_ref[...]) + _dot(xh, wrl_ref[...]) + br_ref[...]
    tm = logits.shape[0]
    lane = lax.broadcasted_iota(jnp.int32, logits.shape, 1)
    neg = jnp.float32(-jnp.inf)
    gmask = lane < N_GROUPS
    gl = jnp.where(gmask, logits, neg)
    gmax = jnp.max(gl, axis=-1, keepdims=True)
    gidx = jnp.min(jnp.where(gmask & (gl == gmax), lane, LANES), axis=-1, keepdims=True)
    gp = 1.0 / jnp.sum(jnp.where(gmask, jnp.exp(gl - gmax), 0.0), axis=-1, keepdims=True)
    emask = (lane >= N_GROUPS) & (lane < N_GROUPS + N_EXPERTS) & ((lane - N_GROUPS) // EXPERTS_PER_GROUP == gidx)
    el = jnp.where(emask, logits, neg)
    v1 = jnp.max(el, axis=-1, keepdims=True)
    i1 = jnp.min(jnp.where(emask & (el == v1), lane, LANES), axis=-1, keepdims=True)
    el2 = jnp.where(lane == i1, neg, el)
    v2 = jnp.max(el2, axis=-1, keepdims=True)
    i2 = jnp.min(jnp.where(emask & (lane != i1) & (el2 == v2), lane, LANES), axis=-1, keepdims=True)
    e21 = jnp.exp(v2 - v1)
    w1 = gp / (1.0 + e21)
    w2 = gp * e21 / (1.0 + e21)

    hot1 = lane == i1
    hot2 = lane == i2
    hot = jnp.where(hot1 | hot2, 1.0, 0.0)
    r = lax.broadcasted_iota(jnp.int32, (tm, tm), 0)
    c = lax.broadcasted_iota(jnp.int32, (tm, tm), 1)
    before = _dot(jnp.where(c < r, 1.0, 0.0).astype(BF16), hot.astype(BF16)) + run_ref[0:1, :]
    rank1 = jnp.sum(jnp.where(hot1, before, 0.0), axis=-1, keepdims=True)
    rank2 = jnp.sum(jnp.where(hot2, before, 0.0), axis=-1, keepdims=True)
    run_ref[0:1, :] = run_ref[0:1, :] + jnp.sum(hot, axis=0, keepdims=True)
    cnt_ref[...] = jnp.broadcast_to(run_ref[0:1, :], cnt_ref.shape)

    ids = jnp.where(lane == 0, (i1 - N_GROUPS).astype(F32), jnp.where(lane == 1, (i2 - N_GROUPS).astype(F32),
                    jnp.where(lane == 2, rank1, jnp.where(lane == 3, rank2, 0.0))))
    ids_ref[...] = jnp.transpose(ids)[0:8, :].astype(jnp.int32)
    wts_ref[...] = jnp.where(lane == 0, w1, jnp.where(lane == 1, w2, 0.0))


def _merge(x2d, ya, yb, yc, wg, bg, wa, wb, wc, wo, lng, lnb, wrh, wrl, br, tm):
    t, d = x2d.shape
    full = lambda shape: pl.BlockSpec(shape, lambda i: (0,) * len(shape))
    row = lambda w: pl.BlockSpec((tm, w), lambda i: (i, 0))
    return pl.pallas_call(
        _merge_kernel,
        grid=(t // tm,),
        in_specs=[row(d), row(A_V_W), row(B_W), row(C_INNER),
                  full(wg.shape), full(bg.shape), full(wa.shape), full(wb.shape), full(wc.shape), full(wo.shape),
                  full((1, d)), full((1, d)), full(wrh.shape), full(wrl.shape), full((1, LANES))],
        out_specs=[row(d), row(d), pl.BlockSpec((8, tm), lambda i: (0, i)), row(LANES), full((8, LANES))],
        out_shape=[jax.ShapeDtypeStruct((t, d), F32), jax.ShapeDtypeStruct((t, d), BF16),
                   jax.ShapeDtypeStruct((8, t), jnp.int32), jax.ShapeDtypeStruct((t, LANES), F32),
                   jax.ShapeDtypeStruct((8, LANES), F32)],
        scratch_shapes=[pltpu.VMEM((8, LANES), F32)],
        compiler_params=pltpu.CompilerParams(dimension_semantics=("arbitrary",), vmem_limit_bytes=VMEM_LIMIT),
        name="merge_ln_router",
    )(x2d, ya, yb, yc, wg, bg, wa, wb, wc, wo, lng, lnb, wrh, wrl, br)


def _experts_kernel(te_ref, nv_ref, xs_ref, wg_ref, wu_ref, wd_ref, ys_in_ref, ys_ref, wgb_ref, wub_ref, wdb_ref,
                    *, first_tile):
    del ys_in_ref
    i = pl.program_id(0)
    tile = i + first_tile

    @pl.when((i == 0) | (te_ref[tile] != te_ref[jnp.maximum(tile - 1, 0)]))
    def _():
        wgb_ref[...] = wg_ref[0, 0].astype(BF16)
        wub_ref[...] = wu_ref[0, 0].astype(BF16)
        wdb_ref[...] = wd_ref[0, 0].astype(BF16)

    @pl.when(tile < nv_ref[0])
    def _():
        xs = xs_ref[...]
        h = _silu(_dot(xs, wgb_ref[...])) * _dot(xs, wub_ref[...])
        ys_ref[...] = _dot(h.astype(BF16), wdb_ref[...]).astype(ys_ref.dtype)

    @pl.when(tile >= nv_ref[0])
    def _():
        ys_ref[...] = jnp.zeros_like(ys_ref)


def _experts(tile_expert, n_valid, xs, wg, wu, wd, layer, tm, first_tile, ys):
    rows, d = xs.shape
    wspec = lambda shape: pl.BlockSpec((1, 1) + shape, lambda i, te, nv: (layer, te[i + first_tile], 0, 0))
    grid_spec = pltpu.PrefetchScalarGridSpec(
        num_scalar_prefetch=2,
        grid=(rows // tm,),
        in_specs=[pl.BlockSpec((tm, d), lambda i, te, nv: (i, 0)),
                  wspec((d, D_EXPERT)), wspec((d, D_EXPERT)), wspec((D_EXPERT, d)),
                  pl.BlockSpec(memory_space=pl.ANY)],
        out_specs=pl.BlockSpec((tm, d), lambda i, te, nv: (i + first_tile, 0)),
        scratch_shapes=[pltpu.VMEM((d, D_EXPERT), BF16), pltpu.VMEM((d, D_EXPERT), BF16),
                        pltpu.VMEM((D_EXPERT, d), BF16)],
    )
    args = (tile_expert, n_valid, xs, wg, wu, wd, ys)
    return pl.pallas_call(
        functools.partial(_experts_kernel, first_tile=first_tile),
        grid_spec=grid_spec,
        out_shape=jax.ShapeDtypeStruct(ys.shape, ys.dtype),
        input_output_aliases={len(args) - 1: 0},
        compiler_params=pltpu.CompilerParams(dimension_semantics=("arbitrary",), vmem_limit_bytes=VMEM_LIMIT),
        name="grouped_experts",
    )(*args)


def _combine_kernel(x_ref, g_ref, wts_ref, lng_ref, lnb_ref, o_ref):
    wts = wts_ref[...]
    ffn = wts[:, 0:1] * g_ref[0].astype(F32) + wts[:, 1:2] * g_ref[1].astype(F32)
    o_ref[...] = _layernorm(DEEPNORM_ALPHA * x_ref[...] + ffn, lng_ref[...], lnb_ref[...])


def _combine(x, g, wts, lng, lnb, tm, first_tile):
    t, d = x.shape
    row = lambda w: pl.BlockSpec((tm, w), lambda i: (i + first_tile, 0))
    vec = pl.BlockSpec((1, d), lambda i: (0, 0))
    return pl.pallas_call(
        _combine_kernel,
        grid=(g.shape[1] // tm,),
        in_specs=[row(d), pl.BlockSpec((2, tm, d), lambda i: (0, i, 0)), row(LANES), vec, vec],
        out_specs=row(d),
        out_shape=jax.ShapeDtypeStruct((t, d), F32),
        input_output_aliases={0: 0},
        compiler_params=pltpu.CompilerParams(dimension_semantics=("parallel",), vmem_limit_bytes=VMEM_LIMIT),
        name="combine_ln",
    )(x, g, wts, lng, lnb)


def _pad_lanes(v, width=LANES):
    v = v.reshape(1, -1).astype(F32)
    return jnp.pad(v, ((0, 0), (0, width - v.shape[1])))


def _rows(a, idx):
    return a.at[idx].get(mode="promise_in_bounds")


def _dispatch(ids, counts, t, tm):
    n_pairs = 2 * t
    n_tiles = n_pairs // tm + N_EXPERTS
    n_rows = n_tiles * tm
    e1, e2, rank1, rank2 = ids[0], ids[1], ids[2], ids[3]
    padded = ((counts + tm - 1) // tm) * tm
    pend = jnp.cumsum(padded)
    pstart = pend - padded
    pos = jnp.concatenate([_rows(pstart, e1) + rank1, _rows(pstart, e2) + rank2])
    count_le = lambda table, v: jnp.sum((table[None, :] <= v[:, None]).astype(jnp.int32), axis=1)
    span = 2 * n_pairs
    tok2 = 2 * jnp.arange(t, dtype=jnp.int32)
    slot = jnp.arange(n_rows - n_pairs, dtype=jnp.int32)
    slot_e = count_le(jnp.cumsum(padded - counts), slot)
    keys = jnp.concatenate([e1 * span + tok2, e2 * span + tok2 + 1, slot_e * span + n_pairs + slot % n_pairs])
    low = lax.sort(keys) % span
    row_tok = jnp.where(low < n_pairs, low // 2, jnp.arange(n_rows, dtype=jnp.int32) % t)
    n_valid = (pend[-1] // tm).astype(jnp.int32).reshape(1)
    tile_start = jnp.arange(n_tiles, dtype=jnp.int32) * tm
    tile_e = jnp.minimum(count_le(pend, tile_start), count_le(pend, pend[-1:] - 1))
    return row_tok, pos, tile_e, n_valid


def _layer(x, p, consts, seq_tile, token_tile, expert_tile):
    bsz, s, d = x.shape
    t = bsz * s
    ya = _mixer_a(x, p["w_a"], p["conv_a"], p["a_log_a"], p["dt_bias_a"], p["norm_a"], consts["expand_a"],
                  seq_tile // A_SEQS, A_SEQS)
    yb = _mixer_b(x, p["w_b"], consts["cos2"], consts["sin2"], consts["ret"], p["norm_b"], seq_tile)
    yc = _mixer_c(x, p["w_c"], p["conv_c"], p["conv_bias_c"], p["dt_bias_c"], p["a_log_c"], p["d_skip_c"],
                  p["norm_c"], consts["expand"], seq_tile)
    x1, x1b, ids, wts, cnt = _merge(x.reshape(t, d), ya.reshape(t, -1), yb.reshape(t, -1), yc.reshape(t, -1),
                                    p["w_gate_in"], p["b_gate"], p["w_branch_a"], p["w_branch_b"],
                                    p["w_branch_c"], p["w_out"], p["ln1_g"], p["ln1_b"], p["w_router_hi"],
                                    p["w_router_lo"], p["b_router"], token_tile)
    counts = cnt[0, N_GROUPS:N_GROUPS + N_EXPERTS].astype(jnp.int32)
    row_tok, pos, tile_e, n_valid = _dispatch(ids, counts, t, expert_tile)
    n_rows = row_tok.shape[0]
    rows_c = n_rows // DISPATCH_CHUNKS
    ys = jnp.zeros((n_rows, d), BF16)
    for k in range(DISPATCH_CHUNKS):
        xs = _rows(x1b, row_tok[k * rows_c:(k + 1) * rows_c])
        ys = _experts(tile_e, n_valid, xs, p["w_gate_e"], p["w_up_e"], p["w_down_e"], p["layer"], expert_tile,
                      k * (rows_c // expert_tile), ys)
    pos = pos.reshape(2, t)
    t_c = t // DISPATCH_CHUNKS
    x2 = x1
    for k in range(DISPATCH_CHUNKS):
        g = _rows(ys, pos[:, k * t_c:(k + 1) * t_c].reshape(-1)).reshape(2, t_c, d)
        x2 = _combine(x2, g, wts, p["ln2_g"], p["ln2_b"], token_tile, k * (t_c // token_tile))
    return x2.reshape(bsz, s, d)


def _layer_params(l, w_in, conv_a, a_log_a, dt_bias_a, norm_a, norm_b, conv_c, conv_bias_c, dt_bias_c,
                  a_log_c, d_skip_c, norm_c, b_gate, w_branch_a, w_branch_b, w_branch_c, w_out,
                  ln1_g, ln1_b, w_router_group, b_router_group, w_router_expert, b_router_expert,
                  w_gate_e, w_up_e, w_down_e, ln2_g, ln2_b):
    o = IN_OFFS
    w = w_in[l]
    d = w.shape[0]
    zpad = lambda n: jnp.zeros((d, n), F32)
    w_a = jnp.concatenate([w[:, o[0]:o[2]], w[:, o[2]:o[3]], zpad(LANES - A_HEADS),
                           w[:, o[3]:o[4]], zpad(LANES - A_HEADS)], axis=1).astype(BF16)
    w_b = w[:, o[4]:o[8]].astype(BF16)
    w_c = jnp.concatenate([w[:, o[8]:o[10]], w[:, o[10]:o[11]], zpad(LANES - C_HEADS)], axis=1).astype(BF16)
    w_router = jnp.concatenate([w_router_group[l], w_router_expert[l],
                                zpad(LANES - N_GROUPS - N_EXPERTS)], axis=1)
    b_router = _pad_lanes(jnp.concatenate([b_router_group[l], b_router_expert[l]]))
    return {
        "w_a": w_a, "w_b": w_b, "w_c": w_c, "w_gate_in": w[:, o[11]:o[12]].astype(BF16),
        "conv_a": conv_a[l], "a_log_a": _pad_lanes(a_log_a[l]), "dt_bias_a": _pad_lanes(dt_bias_a[l]),
        "norm_a": norm_a[l].reshape(1, -1), "norm_b": norm_b[l].reshape(1, -1),
        "conv_c": conv_c[l], "conv_bias_c": conv_bias_c[l].reshape(1, -1),
        "dt_bias_c": _pad_lanes(dt_bias_c[l]), "a_log_c": _pad_lanes(a_log_c[l]),
        "d_skip_c": jnp.repeat(d_skip_c[l], C_HEAD_DIM).reshape(1, -1), "norm_c": norm_c[l].reshape(1, -1),
        "b_gate": b_gate[l],
        "w_branch_a": w_branch_a[l].astype(BF16), "w_branch_b": w_branch_b[l].astype(BF16),
        "w_branch_c": w_branch_c[l].astype(BF16), "w_out": w_out[l].astype(BF16),
        "ln1_g": ln1_g[l].reshape(1, -1), "ln1_b": ln1_b[l].reshape(1, -1),
        "w_router_hi": w_router.astype(BF16),
        "w_router_lo": (w_router - w_router.astype(BF16).astype(F32)).astype(BF16), "b_router": b_router,
        "w_gate_e": w_gate_e, "w_up_e": w_up_e, "w_down_e": w_down_e, "layer": l,
        "ln2_g": ln2_g[l].reshape(1, -1), "ln2_b": ln2_b[l].reshape(1, -1),
    }


def _head_expand(heads, width):
    head_of_lane = jnp.arange(heads * width) // width
    return (jnp.arange(LANES)[:, None] == head_of_lane[None, :]).astype(BF16)


def _forward(x, params, seq_tile=SEQ_TILE, token_tile=TOKEN_TILE, expert_tile=EXPERT_TILE):
    s = x.shape[1]
    cos2, sin2 = _rope_tables(s)
    consts = {"cos2": cos2, "sin2": sin2, "ret": _retention_tables(),
              "expand": _head_expand(C_HEADS, C_HEAD_DIM), "expand_a": _head_expand(A_HEADS, CHUNK)}
    for l in range(DEPTH):
        x = _layer(x, _layer_params(l, *params), consts, seq_tile, token_tile, expert_tile)
    return x


def kernel(x, w_in, conv_a, a_log_a, dt_bias_a, norm_a, norm_b, conv_c, conv_bias_c, dt_bias_c, a_log_c, d_skip_c, norm_c, b_gate, w_branch_a, w_branch_b, w_branch_c, w_out, ln1_g, ln1_b, w_router_group, b_router_group, w_router_expert, b_router_expert, w_gate_e, w_up_e, w_down_e, ln2_g, ln2_b):
    params = (w_in, conv_a, a_log_a, dt_bias_a, norm_a, norm_b, conv_c, conv_bias_c, dt_bias_c, a_log_c,
              d_skip_c, norm_c, b_gate, w_branch_a, w_branch_b, w_branch_c, w_out, ln1_g, ln1_b,
              w_router_group, b_router_group, w_router_expert, b_router_expert, w_gate_e, w_up_e, w_down_e,
              ln2_g, ln2_b)
    return _forward(x, params)
```
